```python
import jax, jax.numpy as jnp
from jax import lax
import numpy as np

D_MODEL = 1024
BATCH = 2
SEQ = 16384
DEPTH = 1

GRID_W = 64
CTX_LEN = 256
RET_HEADS = 4
RET_QK_DIM = 256
RET_V_DIM = 512
RET_CHUNK = 128
CONV_DIM = D_MODEL
CONV_WIDTH = 31
D_FF = 2816
ROPE_BASE = 10000.0
EPS = 1e-6
N_MOD = 9

RET_QK_W = RET_HEADS * RET_QK_DIM
RET_V_W = RET_HEADS * RET_V_DIM
K0 = RET_QK_W
V0 = 2 * RET_QK_W
G0 = V0 + RET_V_W
C0 = G0 + RET_V_W
GA0 = C0 + 2 * CONV_DIM
GB0 = GA0 + D_MODEL
IN_WIDTH = GB0 + D_MODEL
IN_SPLITS = (K0, V0, G0, C0, GA0, GB0)

kernel_name = 'hybrid_retention_conformer_dit'

F32 = jnp.float32


def rmsnorm(x, w):
    xf = x.astype(F32)
    y = xf * lax.rsqrt(jnp.mean(xf * xf, axis=-1, keepdims=True) + EPS)
    return (y * w.astype(F32)).astype(x.dtype)


def layernorm(x, w, b):
    xf = x.astype(F32)
    mu = jnp.mean(xf, axis=-1, keepdims=True)
    var = jnp.mean(jnp.square(xf - mu), axis=-1, keepdims=True)
    y = (xf - mu) * lax.rsqrt(var + EPS)
    return (y * w.astype(F32) + b.astype(F32)).astype(x.dtype)


def modulate(x, w, shift, scale):
    return rmsnorm(x, w) * (1 + scale) + shift


def swiglu(x, w_in, w_out):
    a, b = jnp.split(x @ w_in, 2, axis=-1)
    return (jax.nn.silu(a) * b) @ w_out


def heads(t, d):
    return t.reshape(t.shape[0], t.shape[1], RET_HEADS, d)


def rotate_block(x, ang):
    x1, x2 = jnp.split(x, 2, axis=-1)
    cos = jnp.cos(ang)[None, :, None, :]
    sin = jnp.sin(ang)[None, :, None, :]
    return jnp.concatenate([x1 * cos - x2 * sin, x1 * sin + x2 * cos], axis=-1)


def rope_2d(x, rows, cols):
    half = x.shape[-1] // 2
    inv = ROPE_BASE ** (-jnp.arange(0, half, 2, dtype=F32) / half)
    xr, xc = jnp.split(x.astype(F32), 2, axis=-1)
    out = jnp.concatenate([rotate_block(xr, rows[:, None] * inv[None, :]),
                           rotate_block(xc, cols[:, None] * inv[None, :])], axis=-1)
    return out.astype(x.dtype)


def retention_dir(q, k, v, log_g, state0, strict):
    b, L, h, dk = q.shape
    dv = v.shape[-1]
    n = L // RET_CHUNK
    qc = q.astype(F32).reshape(b, n, RET_CHUNK, h, dk)
    kc = k.astype(F32).reshape(b, n, RET_CHUNK, h, dk)
    vc = v.astype(F32).reshape(b, n, RET_CHUNK, h, dv)
    pos = jnp.arange(RET_CHUNK, dtype=F32)
    diff = pos[:, None] - pos[None, :]
    mask = (diff > 0) if strict else (diff >= 0)
    intra = jnp.where(mask[None], jnp.exp(log_g[:, None, None] * jnp.maximum(diff, 0.0)[None]), 0.0)
    scores = jnp.einsum('bnchd,bnshd->bnhcs', qc, kc) * intra[None, None]
    inner = jnp.einsum('bnhcs,bnshe->bnche', scores, vc)
    q_dec = jnp.exp(log_g[None, :] * (pos[:, None] + 1.0))
    k_dec = jnp.exp(log_g[None, :] * (RET_CHUNK - 1.0 - pos)[:, None])
    c_dec = jnp.exp(log_g * RET_CHUNK)

    def step(state, xs):
        qi, ki, vi = xs
        cross = jnp.einsum('bchd,bhde->bche', qi, state) * q_dec[None, :, :, None]
        upd = jnp.einsum('bshd,bshe->bhde', ki * k_dec[None, :, :, None], vi)
        return state * c_dec[None, :, None, None] + upd, cross

    state, cross = lax.scan(step, state0, (qc.swapaxes(0, 1), kc.swapaxes(0, 1), vc.swapaxes(0, 1)))
    out = (inner + cross.swapaxes(0, 1)).reshape(b, L, h, dv)
    return out.astype(v.dtype), state


def bidir_retention(q, k, v, log_gf, log_gb, rf0, rb0):
    out_f, rf = retention_dir(q, k, v, log_gf, rf0, False)
    out_b, rb = retention_dir(q[:, ::-1], k[:, ::-1], v[:, ::-1], log_gb, rb0, True)
    return out_f + out_b[:, ::-1], rf, rb


def context_state(k, v, log_g, reverse):
    L = k.shape[1]
    pos = jnp.arange(L, dtype=F32)
    expo = pos if reverse else (L - 1.0) - pos
    w = jnp.exp(expo[:, None] * log_g[None, :])
    return jnp.einsum('blhd,blhe->bhde', k.astype(F32) * w[None, :, :, None], v.astype(F32))


def retention_branch(o, gate, gn_w, w_o):
    b, L, h, dv = o.shape
    of = o.astype(F32)
    mu = jnp.mean(of, axis=-1, keepdims=True)
    var = jnp.mean(jnp.square(of - mu), axis=-1, keepdims=True)
    of = ((of - mu) * lax.rsqrt(var + EPS)).reshape(b, L, h * dv) * gn_w.astype(F32)
    return (jax.nn.silu(gate) * of.astype(gate.dtype)) @ w_o


def conv_branch(z, conv_w, conv_b, ln_w, ln_b, w_pw):
    a, g = jnp.split(z, 2, axis=-1)
    y = a * jax.nn.sigmoid(g)
    pad = CONV_WIDTH // 2
    y = lax.conv_general_dilated(y, conv_w[:, None, :].astype(y.dtype), window_strides=(1,),
                                 padding=[(pad, pad)], dimension_numbers=('NWC', 'WIO', 'NWC'),
                                 feature_group_count=CONV_DIM) + conv_b
    y = jax.nn.silu(layernorm(y, ln_w, ln_b))
    return y @ w_pw


def mixer_output(ret_out, rg, cv, ga, gb, gn_w, w_ret_o, conv_w, conv_b, ln_w, ln_b, w_conv_o, w_o):
    y_ret = retention_branch(ret_out, rg, gn_w, w_ret_o)
    y_conv = conv_branch(cv, conv_w, conv_b, ln_w, ln_b, w_conv_o)
    return (jax.nn.sigmoid(ga) * y_ret + jax.nn.sigmoid(gb) * y_conv) @ w_o


def setup_inputs(seed: int = 0) -> dict:
    key = jax.random.key(seed)
    ks = jax.random.split(key, 32)

    def nrm(k, shape, scale):
        return jax.random.normal(k, shape, F32) * scale

    decay_logit = jnp.asarray(np.log(2.0 ** (5.0 + np.arange(RET_HEADS)) - 1.0), F32)
    return {
        'x': nrm(ks[0], (BATCH, SEQ, D_MODEL), 1.0),
        'c': nrm(ks[1], (BATCH, D_MODEL), 1.0),
        'ctx': nrm(ks[2], (BATCH, CTX_LEN, D_MODEL), 1.0),
        'c_ctx': nrm(ks[3], (D_MODEL,), 1.0),
        'w_mod': nrm(ks[4], (DEPTH, D_MODEL, N_MOD * D_MODEL), 0.5 * D_MODEL ** -0.5),
        'b_mod': nrm(ks[5], (DEPTH, N_MOD * D_MODEL), 0.01),
        'norm_ffn1': 1.0 + nrm(ks[6], (DEPTH, D_MODEL), 0.01),
        'w_ffn1_in': nrm(ks[7], (DEPTH, D_MODEL, 2 * D_FF), D_MODEL ** -0.5),
        'w_ffn1_out': nrm(ks[8], (DEPTH, D_FF, D_MODEL), D_FF ** -0.5),
        'norm_mix': 1.0 + nrm(ks[9], (DEPTH, D_MODEL), 0.01),
        'w_in': nrm(ks[10], (DEPTH, D_MODEL, IN_WIDTH), D_MODEL ** -0.5),
        'ret_decay_f': decay_logit[None, :] + nrm(ks[11], (DEPTH, RET_HEADS), 0.1),
        'ret_decay_b': decay_logit[None, :] + nrm(ks[12], (DEPTH, RET_HEADS), 0.1),
        'ret_gn_w': 1.0 + nrm(ks[13], (DEPTH, RET_V_W), 0.01),
        'w_ret_out': nrm(ks[14], (DEPTH, RET_V_W, D_MODEL), RET_V_W ** -0.5),
        'conv_w': nrm(ks[15], (DEPTH, CONV_WIDTH, CONV_DIM), CONV_WIDTH ** -0.5),
        'conv_b': nrm(ks[16], (DEPTH, CONV_DIM), 0.01),
        'conv_ln_w': 1.0 + nrm(ks[17], (DEPTH, CONV_DIM), 0.01),
        'conv_ln_b': nrm(ks[18], (DEPTH, CONV_DIM), 0.01),
        'w_conv_out': nrm(ks[19], (DEPTH, CONV_DIM, D_MODEL), CONV_DIM ** -0.5),
        'w_out': nrm(ks[20], (DEPTH, D_MODEL, D_MODEL), D_MODEL ** -0.5),
        'norm_ffn2': 1.0 + nrm(ks[21], (DEPTH, D_MODEL), 0.01),
        'w_ffn2_in': nrm(ks[22], (DEPTH, D_MODEL, 2 * D_FF), D_MODEL ** -0.5),
        'w_ffn2_out': nrm(ks[23], (DEPTH, D_FF, D_MODEL), D_FF ** -0.5),
        'final_norm': 1.0 + nrm(ks[24], (D_MODEL,), 0.01),
    }


def reference(x, c, ctx, c_ctx, w_mod, b_mod, norm_ffn1, w_ffn1_in, w_ffn1_out, norm_mix, w_in,
              ret_decay_f, ret_decay_b, ret_gn_w, w_ret_out, conv_w, conv_b, conv_ln_w, conv_ln_b,
              w_conv_out, w_out, norm_ffn2, w_ffn2_in, w_ffn2_out, final_norm):
    B, L, _ = x.shape
    ROWS = L // GRID_W
    rows = jnp.repeat(jnp.arange(ROWS, dtype=F32), GRID_W)
    cols = jnp.tile(jnp.arange(GRID_W, dtype=F32), ROWS)
    qk_scale = RET_QK_DIM ** -0.5
    h, hc = x, ctx
    for layer in range(DEPTH):
        last = layer + 1 == DEPTH
        mods = jnp.split((jax.nn.silu(c) @ w_mod[layer] + b_mod[layer])[:, None, :], N_MOD, axis=-1)
        mods_c = jnp.split(jax.nn.silu(c_ctx) @ w_mod[layer] + b_mod[layer], N_MOD, axis=-1)
        sh1, sc1, g1, sh2, sc2, g2, sh3, sc3, g3 = mods
        sh1c, sc1c, g1c, sh2c, sc2c, g2c, sh3c, sc3c, g3c = mods_c

        h = h + 0.5 * g1 * swiglu(modulate(h, norm_ffn1[layer], sh1, sc1), w_ffn1_in[layer], w_ffn1_out[layer])
        hc = hc + 0.5 * g1c * swiglu(modulate(hc, norm_ffn1[layer], sh1c, sc1c), w_ffn1_in[layer], w_ffn1_out[layer])

        u = modulate(h, norm_mix[layer], sh2, sc2)
        uc = modulate(hc, norm_mix[layer], sh2c, sc2c)
        log_gf = jax.nn.log_sigmoid(ret_decay_f[layer].astype(F32))
        log_gb = jax.nn.log_sigmoid(ret_decay_b[layer].astype(F32))
        branch_w = (ret_gn_w[layer], w_ret_out[layer], conv_w[layer], conv_b[layer], conv_ln_w[layer],
                    conv_ln_b[layer], w_conv_out[layer], w_out[layer])

        if last:
            kv_c = uc @ w_in[layer][:, K0:G0]
            kc_, vc_ = jnp.split(kv_c, [RET_QK_W], axis=-1)
            kc_ = heads(kc_, RET_QK_DIM) * qk_scale
            vc_ = heads(vc_, RET_V_DIM)
            rf = context_state(kc_, vc_, log_gf, False)
            rb = context_state(kc_, vc_, log_gb, True)
        else:
            qc_, kc_, vc_, rgc, cvc, gac, gbc = jnp.split(uc @ w_in[layer], IN_SPLITS, axis=-1)
            zeros = jnp.zeros((B, RET_HEADS, RET_QK_DIM, RET_V_DIM), F32)
            oc, rf, rb = bidir_retention(heads(qc_, RET_QK_DIM), heads(kc_, RET_QK_DIM) * qk_scale,
                                         heads(vc_, RET_V_DIM), log_gf, log_gb, zeros, zeros)
            hc = hc + g2c * mixer_output(oc, rgc, cvc, gac, gbc, *branch_w)
            hc = hc + 0.5 * g3c * swiglu(modulate(hc, norm_ffn2[layer], sh3c, sc3c), w_ffn2_in[layer], w_ffn2_out[layer])

        q, k, v, rg, cv, ga, gb = jnp.split(u @ w_in[layer], IN_SPLITS, axis=-1)
        q = rope_2d(heads(q, RET_QK_DIM), rows, cols)
        k = rope_2d(heads(k, RET_QK_DIM), rows, cols) * qk_scale
        v = heads(v, RET_V_DIM)
        o, _, _ = bidir_retention(q, k, v, log_gf, log_gb, rf, rb)
        h = h + g2 * mixer_output(o, rg, cv, ga, gb, *branch_w)

        h = h + 0.5 * g3 * swiglu(modulate(h, norm_ffn2[layer], sh3, sc3), w_ffn2_in[layer], w_ffn2_out[layer])
    return rmsnorm(h, final_norm)
```

```python
import functools

import jax
import jax.numpy as jnp
from jax import lax
from jax.experimental import pallas as pl
from jax.experimental.pallas import tpu as pltpu

F32 = jnp.float32
BF16 = jnp.bfloat16

D_MODEL = 1024
GRID_W = 64
RET_HEADS = 4
RET_QK_DIM = 256
RET_V_DIM = 512
RET_CHUNK = 128
CONV_WIDTH = 31
CONV_PAD = CONV_WIDTH // 2
D_FF = 2816
ROPE_BASE = 10000.0
EPS = 1e-6
N_MOD = 9

RET_QK_W = RET_HEADS * RET_QK_DIM
RET_V_W = RET_HEADS * RET_V_DIM
K0 = RET_QK_W
V0 = 2 * RET_QK_W
G0 = V0 + RET_V_W
C0 = G0 + RET_V_W
GA0 = C0 + 2 * D_MODEL
GB0 = GA0 + D_MODEL
IN_WIDTH = GB0 + D_MODEL

V7X_VMEM_LIMIT_BYTES = 56 * 1024 * 1024
SUBLANES = 8
LANES = 128
BF16_ROWS = 16

FFN_CHUNK = 256
FFN_TM = 512
INPROJ_TM = 256
MIX_TM = 256
RET_T = 512


def _resident(shape):
    nd = len(shape)
    return pl.BlockSpec(shape, lambda *_: (0,) * nd, pipeline_mode=pl.Buffered(1))


def _params(sem):
    return pltpu.CompilerParams(dimension_semantics=sem, vmem_limit_bytes=V7X_VMEM_LIMIT_BYTES)


def _rms_mod(x, w, scale, shift):
    y = x * lax.rsqrt(jnp.mean(x * x, axis=-1, keepdims=True) + EPS)
    return (y * w) * (1.0 + scale) + shift


def _mods_kernel(c_ref, w_ref, b_ref, o_ref):
    c = c_ref[...]
    s = c * jax.nn.sigmoid(c)
    o_ref[...] = jnp.dot(s, w_ref[...], preferred_element_type=F32,
                         precision=lax.Precision.HIGHEST) + b_ref[...]


def _mods(cc, w_mod, b_mod):
    rows, d = cc.shape
    n = w_mod.shape[1]
    tn = D_MODEL
    return pl.pallas_call(
        _mods_kernel,
        grid=(n // tn,),
        in_specs=[pl.BlockSpec((rows, d), lambda j: (0, 0)),
                  pl.BlockSpec((d, tn), lambda j: (0, j)),
                  pl.BlockSpec((1, tn), lambda j: (0, j))],
        out_specs=pl.BlockSpec((rows, tn), lambda j: (0, j)),
        out_shape=jax.ShapeDtypeStruct((rows, n), F32),
        compiler_params=_params(("arbitrary",)),
        name="mods",
    )(cc, w_mod, b_mod.reshape(1, n))


def _ffn_kernel(x_ref, mods_ref, nw_ref, w1_ref, w2_ref, n2w_ref, *out_refs, row0, final):
    x = x_ref[0]
    sh = mods_ref[0, row0:row0 + 1, :]
    sc = mods_ref[0, row0 + 1:row0 + 2, :]
    g = mods_ref[0, row0 + 2:row0 + 3, :]
    xm = _rms_mod(x, nw_ref[...], sc, sh).astype(BF16)
    nch, _, ck2 = w1_ref.shape
    ck = ck2 // 2
    acc = jnp.zeros_like(x)
    for j in range(nch):
        hc = jnp.dot(xm, w1_ref[j], preferred_element_type=F32)
        a = hc[:, :ck]
        b = hc[:, ck:]
        act = (a * jax.nn.sigmoid(a) * b).astype(BF16)
        acc = acc + jnp.dot(act, w2_ref[j], preferred_element_type=F32)
    h = x + (0.5 * g) * acc
    if final:
        (o_ref,) = out_refs
        y = h * lax.rsqrt(jnp.mean(h * h, axis=-1, keepdims=True) + EPS)
        o_ref[0] = y * n2w_ref[...]
    else:
        h_ref, u_ref = out_refs
        h_ref[0] = h
        sh2 = mods_ref[0, row0 + 3:row0 + 4, :]
        sc2 = mods_ref[0, row0 + 4:row0 + 5, :]
        u_ref[0] = _rms_mod(h, n2w_ref[...], sc2, sh2).astype(BF16)


def _ffn(x, mods, norm_w, w1, w2, norm2_w, *, row0, final, tm):
    b, l, d = x.shape
    tm = min(tm, l)
    mods_map = (lambda i, t: (i, 0, 0)) if mods.shape[0] == b else (lambda i, t: (0, 0, 0))
    tok = lambda i, t: (i, t, 0)
    in_specs = [pl.BlockSpec((1, tm, d), tok),
                pl.BlockSpec((1, N_MOD, d), mods_map),
                _resident((1, d)),
                _resident(w1.shape),
                _resident(w2.shape),
                _resident((1, d))]
    if final:
        out_specs = pl.BlockSpec((1, tm, d), tok)
        out_shape = jax.ShapeDtypeStruct((b, l, d), F32)
    else:
        out_specs = [pl.BlockSpec((1, tm, d), tok), pl.BlockSpec((1, tm, d), tok)]
        out_shape = [jax.ShapeDtypeStruct((b, l, d), F32), jax.ShapeDtypeStruct((b, l, d), BF16)]
    return pl.pallas_call(
        functools.partial(_ffn_kernel, row0=row0, final=final),
        grid=(b, l // tm),
        in_specs=in_specs,
        out_specs=out_specs,
        out_shape=out_shape,
        compiler_params=_params(("parallel", "parallel")),
        name="ffn_final" if final else "ffn_mix",
    )(x, mods, norm_w.reshape(1, d), w1, w2, norm2_w.reshape(1, d))


def _ffn_weights(w_in, w_out):
    d = w_in.shape[0]
    nch = D_FF // FFN_CHUNK
    a = w_in[:, :D_FF].reshape(d, nch, FFN_CHUNK)
    b = w_in[:, D_FF:].reshape(d, nch, FFN_CHUNK)
    w1 = jnp.concatenate([a, b], axis=-1).transpose(1, 0, 2).astype(BF16)
    w2 = w_out.reshape(nch, FFN_CHUNK, d).astype(BF16)
    return w1, w2


def _ctx_state_kernel(lgf_ref, lgb_ref, u_ref, wk_ref, wv_ref, rf_ref, rb_ref):
    h = pl.program_id(1)
    u = u_ref[0]
    n = u.shape[0]
    k = jnp.dot(u, wk_ref[...], preferred_element_type=F32) * (RET_QK_DIM ** -0.5)
    v = jnp.dot(u, wv_ref[...], preferred_element_type=F32).astype(BF16)
    pos = lax.broadcasted_iota(jnp.int32, (n, 1), 0).astype(F32)
    wf = jnp.exp(lgf_ref[h] * ((n - 1.0) - pos))
    wb = jnp.exp(lgb_ref[h] * pos)
    tdot = lambda a, c: lax.dot_general(a, c, (((0,), (0,)), ((), ())), preferred_element_type=F32)
    rf_ref[0, 0] = tdot((k * wf).astype(BF16), v)
    rb_ref[0, 0] = tdot((k * wb).astype(BF16), v)


def _ctx_state(uc, w_in_bf, lgf, lgb):
    b, n, d = uc.shape
    smem = pl.BlockSpec(memory_space=pltpu.SMEM)
    st = jax.ShapeDtypeStruct((b, RET_HEADS, RET_QK_DIM, RET_V_DIM), F32)
    st_spec = pl.BlockSpec((1, 1, RET_QK_DIM, RET_V_DIM), lambda i, h: (i, h, 0, 0))
    return pl.pallas_call(
        _ctx_state_kernel,
        grid=(b, RET_HEADS),
        in_specs=[smem, smem,
                  pl.BlockSpec((1, n, d), lambda i, h: (i, 0, 0)),
                  pl.BlockSpec((d, RET_QK_DIM), lambda i, h: (0, K0 // RET_QK_DIM + h)),
                  pl.BlockSpec((d, RET_V_DIM), lambda i, h: (0, V0 // RET_V_DIM + h))],
        out_specs=[st_spec, st_spec],
        out_shape=[st, st],
        compiler_params=_params(("parallel", "parallel")),
        name="ctx_state",
    )(lgf, lgb, uc, w_in_bf, w_in_bf)


def _inproj_kernel(u_ref, w_ref, cos_ref, sin_ref, q_ref, k_ref, v_ref, rg_ref, y_ref, ga_ref, gb_ref):
    u = u_ref[0]
    cos = cos_ref[...]
    sin = sin_ref[...]

    def proj(c0, c1):
        return jnp.dot(u, w_ref[:, c0:c1], preferred_element_type=F32)

    def rope_store(val, ref, scale):
        for c0 in range(0, RET_QK_W, LANES):
            t0 = c0 % RET_QK_DIM
            slab = val[:, c0:c0 + LANES]
            r = slab * cos[:, t0:t0 + LANES] + pltpu.roll(slab, LANES // 2, 1) * sin[:, t0:t0 + LANES]
            ref[0, :, c0:c0 + LANES] = (r * scale).astype(BF16)

    rope_store(proj(0, K0), q_ref, 1.0)
    rope_store(proj(K0, V0), k_ref, RET_QK_DIM ** -0.5)
    v_ref[0] = proj(V0, G0).astype(BF16)
    rg = proj(G0, C0)
    rg_ref[0] = (rg * jax.nn.sigmoid(rg)).astype(BF16)
    y_ref[0] = (proj(C0, C0 + D_MODEL) * jax.nn.sigmoid(proj(C0 + D_MODEL, GA0))).astype(BF16)
    ga_ref[0] = jax.nn.sigmoid(proj(GA0, GB0)).astype(BF16)
    gb_ref[0] = jax.nn.sigmoid(proj(GB0, IN_WIDTH)).astype(BF16)


def _inproj(u, w_in_bf, cos_tab, sin_tab, *, tm):
    b, l, d = u.shape
    tok = lambda i, t: (i, t, 0)
    tab = lambda i, t: (t, 0)
    widths = (RET_QK_W, RET_QK_W, RET_V_W, RET_V_W, D_MODEL, D_MODEL, D_MODEL)
    return pl.pallas_call(
        _inproj_kernel,
        grid=(b, l // tm),
        in_specs=[pl.BlockSpec((1, tm, d), tok),
                  _resident(w_in_bf.shape),
                  pl.BlockSpec((tm, RET_QK_DIM), tab),
                  pl.BlockSpec((tm, RET_QK_DIM), tab)],
        out_specs=[pl.BlockSpec((1, tm, w), tok) for w in widths],
        out_shape=[jax.ShapeDtypeStruct((b, l, w), BF16) for w in widths],
        compiler_params=_params(("parallel", "parallel")),
        name="inproj",
    )(u, w_in_bf, cos_tab, sin_tab)


def _rope_tables(l):
    half = RET_QK_DIM // 2
    inv = ROPE_BASE ** (-jnp.arange(0, half, 2, dtype=F32) / half)
    rows = jnp.repeat(jnp.arange(l // GRID_W, dtype=F32), GRID_W)
    cols = jnp.tile(jnp.arange(GRID_W, dtype=F32), l // GRID_W)
    ar = rows[:, None] * inv[None, :]
    ac = cols[:, None] * inv[None, :]
    cos_tab = jnp.concatenate([jnp.cos(ar), jnp.cos(ar), jnp.cos(ac), jnp.cos(ac)], axis=-1)
    sin_tab = jnp.concatenate([-jnp.sin(ar), jnp.sin(ar), -jnp.sin(ac), jnp.sin(ac)], axis=-1)
    return cos_tab, sin_tab


def _ret_kernel(lg_ref, q_ref, k_ref, v_ref, s0_ref, *rest, reverse):
    if reverse:
        o_ref, state = rest
    else:
        ob_ref, rg_ref, gnw_ref, o_ref, state = rest
    h = pl.program_id(1)

    @pl.when(pl.program_id(2) == 0)
    def _():
        state[...] = s0_ref[0, 0]

    c = RET_CHUNK
    lg = lg_ref[h]
    diff = (lax.broadcasted_iota(jnp.int32, (c, c), 0)
            - lax.broadcasted_iota(jnp.int32, (c, c), 1)).astype(F32)
    pos = lax.broadcasted_iota(jnp.int32, (c, 1), 0).astype(F32)
    if reverse:
        dmat = jnp.where(diff < 0, jnp.exp(lg * jnp.maximum(-diff, 0.0)), 0.0)
        qdec = jnp.exp(lg * (c - pos))
        kdec = jnp.exp(lg * pos)
    else:
        dmat = jnp.where(diff >= 0, jnp.exp(lg * jnp.maximum(diff, 0.0)), 0.0)
        qdec = jnp.exp(lg * (pos + 1.0))
        kdec = jnp.exp(lg * ((c - 1.0) - pos))
    cdec = jnp.exp(lg * jnp.full((1, 1), float(c), F32))

    nchunk = q_ref.shape[1] // c
    for ci in (reversed(range(nchunk)) if reverse else range(nchunk)):
        rows = slice(ci * c, (ci + 1) * c)
        qc = q_ref[0, rows, :]
        kc = k_ref[0, rows, :]
        vc = v_ref[0, rows, :]
        s = lax.dot_general(qc, kc, (((1,), (1,)), ((), ())), preferred_element_type=F32)
        inner = jnp.dot((s * dmat).astype(BF16), vc, preferred_element_type=F32)
        st = state[...]
        cross = jnp.dot(qc, st.astype(BF16), preferred_element_type=F32) * qdec
        o = inner + cross
        kd = (kc.astype(F32) * kdec).astype(BF16)
        upd = lax.dot_general(kd, vc, (((0,), (0,)), ((), ())), preferred_element_type=F32)
        state[...] = st * cdec + upd
        if reverse:
            o_ref[0, rows, :] = o
        else:
            o = o + ob_ref[0, rows, :]
            mu = jnp.mean(o, axis=-1, keepdims=True)
            dlt = o - mu
            var = jnp.mean(dlt * dlt, axis=-1, keepdims=True)
            on = (dlt * lax.rsqrt(var + EPS)) * gnw_ref[...]
            o_ref[0, rows, :] = (rg_ref[0, rows, :].astype(F32) * on).astype(BF16)


def _retention(lg, q, k, v, s0, extra, *, reverse, t):
    b, l, _ = q.shape
    nt = l // t
    tmap = (lambda i, h, j: (i, nt - 1 - j, h)) if reverse else (lambda i, h, j: (i, j, h))
    in_specs = [pl.BlockSpec(memory_space=pltpu.SMEM),
                pl.BlockSpec((1, t, RET_QK_DIM), tmap),
                pl.BlockSpec((1, t, RET_QK_DIM), tmap),
                pl.BlockSpec((1, t, RET_V_DIM), tmap),
                pl.BlockSpec((1, 1, RET_QK_DIM, RET_V_DIM), lambda i, h, j: (i, h, 0, 0))]
    if reverse:
        out_dtype = F32
    else:
        in_specs += [pl.BlockSpec((1, t, RET_V_DIM), tmap),
                     pl.BlockSpec((1, t, RET_V_DIM), tmap),
                     pl.BlockSpec((1, RET_V_DIM), lambda i, h, j: (0, h))]
        out_dtype = BF16
    return pl.pallas_call(
        functools.partial(_ret_kernel, reverse=reverse),
        grid=(b, RET_HEADS, nt),
        in_specs=in_specs,
        out_specs=pl.BlockSpec((1, t, RET_V_DIM), tmap),
        out_shape=jax.ShapeDtypeStruct((b, l, RET_V_W), out_dtype),
        scratch_shapes=[pltpu.VMEM((RET_QK_DIM, RET_V_DIM), F32)],
        compiler_params=_params(("parallel", "parallel", "arbitrary")),
        name="ret_bwd" if reverse else "ret_fwd",
    )(lg, q, k, v, s0, *extra)


def _mixer_kernel(yrp_ref, y_ref, yprev_ref, ynext_ref, ga_ref, gb_ref, h_ref, mods_ref,
                  wro_ref, cw_ref, cb_ref, lnw_ref, lnb_ref, wco_ref, wo_ref, o_ref, ybuf):
    t = pl.program_id(1)
    tm = y_ref.shape[1]
    hb = BF16_ROWS
    ybuf[0:hb, :] = jnp.where(t > 0, yprev_ref[0].astype(F32), 0.0)
    ybuf[hb:hb + tm, :] = y_ref[0].astype(F32)
    ybuf[hb + tm:2 * hb + tm, :] = jnp.where(t < pl.num_programs(1) - 1, ynext_ref[0].astype(F32), 0.0)
    acc = jnp.zeros((tm, D_MODEL), F32)
    for j in range(CONV_WIDTH):
        off = hb - CONV_PAD + j
        acc = acc + ybuf[off:off + tm, :] * cw_ref[j:j + 1, :]
    yc = acc + cb_ref[...]
    mu = jnp.mean(yc, axis=-1, keepdims=True)
    dlt = yc - mu
    var = jnp.mean(dlt * dlt, axis=-1, keepdims=True)
    yn = (dlt * lax.rsqrt(var + EPS)) * lnw_ref[...] + lnb_ref[...]
    ys = (yn * jax.nn.sigmoid(yn)).astype(BF16)
    y_conv = jnp.dot(ys, wco_ref[...], preferred_element_type=F32)
    y_ret = jnp.dot(yrp_ref[0], wro_ref[...], preferred_element_type=F32)
    merged = (ga_ref[0].astype(F32) * y_ret + gb_ref[0].astype(F32) * y_conv).astype(BF16)
    m = jnp.dot(merged, wo_ref[...], preferred_element_type=F32)
    g2 = mods_ref[0, 5:6, :]
    o_ref[0] = h_ref[0] + g2 * m


def _mixer(yrp, y, sga, sgb, h, mods, wro, conv_w, conv_b, ln_w, ln_b, wco, wo, *, tm):
    b, l, d = h.shape
    nt = l // tm
    hb = BF16_ROWS
    tok = lambda i, t: (i, t, 0)
    prev = lambda i, t: (i, jnp.maximum(t * (tm // hb) - 1, 0), 0)
    nxt = lambda i, t: (i, jnp.minimum((t + 1) * (tm // hb), l // hb - 1), 0)
    vec = lambda a: a.reshape(1, d)
    return pl.pallas_call(
        _mixer_kernel,
        grid=(b, nt),
        in_specs=[pl.BlockSpec((1, tm, RET_V_W), tok),
                  pl.BlockSpec((1, tm, d), tok),
                  pl.BlockSpec((1, hb, d), prev),
                  pl.BlockSpec((1, hb, d), nxt),
                  pl.BlockSpec((1, tm, d), tok),
                  pl.BlockSpec((1, tm, d), tok),
                  pl.BlockSpec((1, tm, d), tok),
                  pl.BlockSpec((1, N_MOD, d), lambda i, t: (i, 0, 0)),
                  _resident(wro.shape),
                  _resident(conv_w.shape),
                  _resident((1, d)), _resident((1, d)), _resident((1, d)),
                  _resident(wco.shape),
                  _resident(wo.shape)],
        out_specs=pl.BlockSpec((1, tm, d), tok),
        out_shape=jax.ShapeDtypeStruct((b, l, d), F32),
        scratch_shapes=[pltpu.VMEM((tm + 2 * hb, d), F32)],
        compiler_params=_params(("parallel", "parallel")),
        name="mixer",
    )(yrp, y, y, y, sga, sgb, h, mods, wro, conv_w, vec(conv_b), vec(ln_w), vec(ln_b), wco, wo)


def kernel(x, c, ctx, c_ctx, w_mod, b_mod, norm_ffn1, w_ffn1_in, w_ffn1_out, norm_mix, w_in, ret_decay_f, ret_decay_b, ret_gn_w, w_ret_out, conv_w, conv_b, conv_ln_w, conv_ln_b, w_conv_out, w_out, norm_ffn2, w_ffn2_in, w_ffn2_out, final_norm):
    assert w_mod.shape[0] == 1, "single-layer block"
    b, l, d = x.shape
    assert l % max(FFN_TM, RET_T) == 0 and l % GRID_W == 0 and d == D_MODEL

    cc = jnp.zeros((SUBLANES, d), F32).at[:b].set(c).at[b].set(c_ctx)
    mods_all = _mods(cc, w_mod[0], b_mod[0])
    mods = mods_all[:b].reshape(b, N_MOD, d)
    mods_c = mods_all[b:b + 1].reshape(1, N_MOD, d)

    w1a, w1b = _ffn_weights(w_ffn1_in[0], w_ffn1_out[0])
    w2a, w2b = _ffn_weights(w_ffn2_in[0], w_ffn2_out[0])
    w_in_bf = w_in[0].astype(BF16)
    lgf = jax.nn.log_sigmoid(ret_decay_f[0].astype(F32))
    lgb = jax.nn.log_sigmoid(ret_decay_b[0].astype(F32))

    h, u = _ffn(x, mods, norm_ffn1[0], w1a, w1b, norm_mix[0], row0=0, final=False, tm=FFN_TM)
    _, uc = _ffn(ctx, mods_c, norm_ffn1[0], w1a, w1b, norm_mix[0], row0=0, final=False, tm=FFN_TM)

    rf, rb = _ctx_state(uc, w_in_bf, lgf, lgb)

    cos_tab, sin_tab = _rope_tables(l)
    q, k, v, srg, y, sga, sgb = _inproj(u, w_in_bf, cos_tab, sin_tab, tm=INPROJ_TM)

    o_b = _retention(lgb, q, k, v, rb, (), reverse=True, t=RET_T)
    yrp = _retention(lgf, q, k, v, rf, (o_b, srg, ret_gn_w[0].reshape(1, RET_V_W)), reverse=False, t=RET_T)

    h2 = _mixer(yrp, y, sga, sgb, h, mods, w_ret_out[0].astype(BF16), conv_w[0], conv_b[0],
                conv_ln_w[0], conv_ln_b[0], w_conv_out[0].astype(BF16), w_out[0].astype(BF16), tm=MIX_TM)

    return _ffn(h2, mods, norm_ffn2[0], w2a, w2b, final_norm, row0=6, final=True, tm=FFN_TM)
```

```python
import functools

import jax
import jax.numpy as jnp
from jax import lax
from jax.experimental import pallas as pl
from jax.experimental.pallas import tpu as pltpu

F32 = jnp.float32
BF16 = jnp.bfloat16

D_MODEL = 1024
GRID_W = 64
RET_HEADS = 4
RET_QK_DIM = 256
RET_V_DIM = 512
CONV_WIDTH = 31
CONV_PAD = CONV_WIDTH // 2
D_FF = 2816
ROPE_BASE = 10000.0
EPS = 1e-6
N_MOD = 9

RET_QK_W = RET_HEADS * RET_QK_DIM
RET_V_W = RET_HEADS * RET_V_DIM
K0 = RET_QK_W
V0 = 2 * RET_QK_W
G0 = V0 + RET_V_W
C0 = G0 + RET_V_W
GA0 = C0 + 2 * D_MODEL
GB0 = GA0 + D_MODEL
IN_WIDTH = GB0 + D_MODEL

V7X_VMEM_LIMIT_BYTES = 56 * 1024 * 1024
SUBLANES = 8
LANES = 128
BF16_ROWS = 16
HALO = 2 * BF16_ROWS

FFN_CHUNK = 256
FFN_TM = 512
INPROJ_TM = 256
MIX_TM = 512
RET_T = 1024
RET_SUB = 256
CONV_RB = 64


def _resident(shape):
    nd = len(shape)
    return pl.BlockSpec(shape, lambda *_: (0,) * nd, pipeline_mode=pl.Buffered(1))


def _params(sem):
    return pltpu.CompilerParams(dimension_semantics=sem, vmem_limit_bytes=V7X_VMEM_LIMIT_BYTES)


def _rms_mod(x, w, scale, shift):
    y = x * lax.rsqrt(jnp.mean(x * x, axis=-1, keepdims=True) + EPS)
    return (y * w) * (1.0 + scale) + shift


def _layernorm(x, w, b):
    mu = jnp.mean(x, axis=-1, keepdims=True)
    d = x - mu
    var = jnp.mean(d * d, axis=-1, keepdims=True)
    return (d * lax.rsqrt(var + EPS)) * w + b


def _mods_kernel(c_ref, w_ref, b_ref, o_ref):
    c = c_ref[...]
    s = c * jax.nn.sigmoid(c)
    o_ref[...] = jnp.dot(s, w_ref[...], preferred_element_type=F32,
                         precision=lax.Precision.HIGHEST) + b_ref[...]


def _mods(cc, w_mod, b_mod):
    rows, d = cc.shape
    n = w_mod.shape[1]
    tn = D_MODEL
    return pl.pallas_call(
        _mods_kernel,
        grid=(n // tn,),
        in_specs=[pl.BlockSpec((rows, d), lambda j: (0, 0)),
                  pl.BlockSpec((d, tn), lambda j: (0, j)),
                  pl.BlockSpec((1, tn), lambda j: (0, j))],
        out_specs=pl.BlockSpec((rows, tn), lambda j: (0, j)),
        out_shape=jax.ShapeDtypeStruct((rows, n), F32),
        compiler_params=_params(("arbitrary",)),
        name="mods",
    )(cc, w_mod, b_mod.reshape(1, n))


def _ffn_kernel(x_ref, mods_ref, nw_ref, w1_ref, w2_ref, n2w_ref, *out_refs, row0, final):
    x = x_ref[0]
    sh = mods_ref[0, row0:row0 + 1, :]
    sc = mods_ref[0, row0 + 1:row0 + 2, :]
    g = mods_ref[0, row0 + 2:row0 + 3, :]
    xm = _rms_mod(x, nw_ref[...], sc, sh).astype(BF16)
    ck = FFN_CHUNK
    acc = jnp.zeros_like(x)
    for c0 in range(0, D_FF, ck):
        a = jnp.dot(xm, w1_ref[:, c0:c0 + ck], preferred_element_type=F32)
        b = jnp.dot(xm, w1_ref[:, D_FF + c0:D_FF + c0 + ck], preferred_element_type=F32)
        act = (a * jax.nn.sigmoid(a) * b).astype(BF16)
        acc = acc + jnp.dot(act, w2_ref[c0:c0 + ck, :], preferred_element_type=F32)
    h = x + (0.5 * g) * acc
    if final:
        (o_ref,) = out_refs
        y = h * lax.rsqrt(jnp.mean(h * h, axis=-1, keepdims=True) + EPS)
        o_ref[0] = y * n2w_ref[...]
    else:
        h_ref, u_ref = out_refs
        h_ref[0] = h
        sh2 = mods_ref[0, row0 + 3:row0 + 4, :]
        sc2 = mods_ref[0, row0 + 4:row0 + 5, :]
        u_ref[0] = _rms_mod(h, n2w_ref[...], sc2, sh2).astype(BF16)


def _ffn(x, mods, norm_w, w1, w2, norm2_w, *, row0, final, tm):
    b, l, d = x.shape
    tm = min(tm, l)
    mods_map = (lambda i, t: (i, 0, 0)) if mods.shape[0] == b else (lambda i, t: (0, 0, 0))
    tok = lambda i, t: (i, t, 0)
    in_specs = [pl.BlockSpec((1, tm, d), tok),
                pl.BlockSpec((1, N_MOD, d), mods_map),
                _resident((1, d)),
                _resident(w1.shape),
                _resident(w2.shape),
                _resident((1, d))]
    if final:
        out_specs = pl.BlockSpec((1, tm, d), tok)
        out_shape = jax.ShapeDtypeStruct((b, l, d), F32)
    else:
        out_specs = [pl.BlockSpec((1, tm, d), tok), pl.BlockSpec((1, tm, d), tok)]
        out_shape = [jax.ShapeDtypeStruct((b, l, d), F32), jax.ShapeDtypeStruct((b, l, d), BF16)]
    return pl.pallas_call(
        functools.partial(_ffn_kernel, row0=row0, final=final),
        grid=(b, l // tm),
        in_specs=in_specs,
        out_specs=out_specs,
        out_shape=out_shape,
        compiler_params=_params(("parallel", "parallel")),
        name="ffn_final" if final else "ffn_mix",
    )(x, mods, norm_w.reshape(1, d), w1, w2, norm2_w.reshape(1, d))


def _ctx_state_kernel(lgf_ref, lgb_ref, u_ref, wk_ref, wv_ref, rf_ref, rb_ref):
    h = pl.program_id(1)
    u = u_ref[0]
    n = u.shape[0]
    k = jnp.dot(u, wk_ref[...], preferred_element_type=F32)
    v = jnp.dot(u, wv_ref[...], preferred_element_type=F32).astype(BF16)
    pos = lax.broadcasted_iota(jnp.int32, (n, 1), 0).astype(F32)
    wf = jnp.exp(lgf_ref[h] * ((n - 1.0) - pos))
    wb = jnp.exp(lgb_ref[h] * pos)
    tdot = lambda a, c: lax.dot_general(a, c, (((0,), (0,)), ((), ())), preferred_element_type=F32)
    rf_ref[0, 0] = tdot((k * wf).astype(BF16), v)
    rb_ref[0, 0] = tdot((k * wb).astype(BF16), v)


def _ctx_state(uc, w_in_bf, lgf, lgb):
    b, n, d = uc.shape
    smem = pl.BlockSpec(memory_space=pltpu.SMEM)
    st = jax.ShapeDtypeStruct((b, RET_HEADS, RET_QK_DIM, RET_V_DIM), F32)
    st_spec = pl.BlockSpec((1, 1, RET_QK_DIM, RET_V_DIM), lambda i, h: (i, h, 0, 0))
    return pl.pallas_call(
        _ctx_state_kernel,
        grid=(b, RET_HEADS),
        in_specs=[smem, smem,
                  pl.BlockSpec((1, n, d), lambda i, h: (i, 0, 0)),
                  pl.BlockSpec((d, RET_QK_DIM), lambda i, h: (0, K0 // RET_QK_DIM + h)),
                  pl.BlockSpec((d, RET_V_DIM), lambda i, h: (0, V0 // RET_V_DIM + h))],
        out_specs=[st_spec, st_spec],
        out_shape=[st, st],
        compiler_params=_params(("parallel", "parallel")),
        name="ctx_state",
    )(lgf, lgb, uc, w_in_bf, w_in_bf)


def _depthwise_conv(ybuf, cw_ref, cbuf, c0):
    tm = cbuf.shape[0]
    rows = CONV_RB + HALO
    base = BF16_ROWS - CONV_PAD
    cs = slice(c0, c0 + LANES)
    for r0 in range(0, tm, CONV_RB):
        src = ybuf[r0:r0 + rows, cs]
        acc = jnp.zeros((CONV_RB, LANES), F32)
        for r in range(SUBLANES):
            zr = src if r == 0 else pltpu.roll(src, rows - r, 0)
            for a0 in range(0, HALO, SUBLANES):
                j = a0 + r - base
                if 0 <= j < CONV_WIDTH:
                    acc = acc + zr[a0:a0 + CONV_RB, :] * cw_ref[j:j + 1, cs]
        cbuf[r0:r0 + CONV_RB, cs] = acc


def _inproj_kernel(u_ref, uprev_ref, unext_ref, w_ref, rcos_ref, rsin_ref, ccos_ref, csin_ref,
                   cw_ref, cb_ref, lnw_ref, lnb_ref,
                   q_ref, k_ref, v_ref, rg_ref, ys_ref, ga_ref, gb_ref, uext, ybuf, cbuf):
    t = pl.program_id(1)
    tm = u_ref.shape[1]
    hb = BF16_ROWS
    u = u_ref[0]

    def proj(c0, c1):
        return jnp.dot(u, w_ref[:, c0:c1], preferred_element_type=F32)

    def rope_store(val, ref):
        for gi in range(tm // GRID_W):
            rs = slice(gi * GRID_W, (gi + 1) * GRID_W)
            for c0 in range(0, RET_QK_W, LANES):
                if c0 % RET_QK_DIM == 0:
                    cs, sn = rcos_ref[0, gi:gi + 1, :], rsin_ref[0, gi:gi + 1, :]
                else:
                    cs, sn = ccos_ref[...], csin_ref[...]
                slab = val[rs, c0:c0 + LANES]
                ref[0, rs, c0:c0 + LANES] = (slab * cs + pltpu.roll(slab, LANES // 2, 1) * sn).astype(BF16)

    uext[0:hb, :] = jnp.where(t > 0, uprev_ref[0], jnp.zeros_like(uprev_ref[0]))
    uext[hb:hb + tm, :] = u
    uext[hb + tm:, :] = jnp.where(t < pl.num_programs(1) - 1, unext_ref[0], jnp.zeros_like(unext_ref[0]))
    ue = uext[...]
    ca = jnp.dot(ue, w_ref[:, C0:C0 + D_MODEL], preferred_element_type=F32)
    cg = jnp.dot(ue, w_ref[:, C0 + D_MODEL:GA0], preferred_element_type=F32)
    ybuf[...] = ca * jax.nn.sigmoid(cg)

    def store_v(c0):
        v_ref[0, :, c0 - V0:c0 - V0 + D_MODEL] = proj(c0, c0 + D_MODEL).astype(BF16)

    def store_rg(c0):
        rg = proj(c0, c0 + D_MODEL)
        rg_ref[0, :, c0 - G0:c0 - G0 + D_MODEL] = (rg * jax.nn.sigmoid(rg)).astype(BF16)

    def store_gate(c0, ref):
        ref[0] = jax.nn.sigmoid(proj(c0, c0 + D_MODEL)).astype(BF16)

    pieces = (lambda: rope_store(proj(0, K0), q_ref),
              lambda: rope_store(proj(K0, V0), k_ref),
              lambda: store_v(V0), lambda: store_v(V0 + D_MODEL),
              lambda: store_rg(G0), lambda: store_rg(G0 + D_MODEL),
              lambda: store_gate(GA0, ga_ref), lambda: store_gate(GB0, gb_ref))
    for i, piece in enumerate(pieces):
        _depthwise_conv(ybuf, cw_ref, cbuf, i * LANES)
        piece()

    yn = _layernorm(cbuf[...] + cb_ref[...], lnw_ref[...], lnb_ref[...])
    ys_ref[0] = (yn * jax.nn.sigmoid(yn)).astype(BF16)


def _inproj(u, w_in_bf, rope, conv_w, conv_b, ln_w, ln_b, *, tm):
    b, l, d = u.shape
    hb = BF16_ROWS
    gpt = tm // GRID_W
    rcos, rsin, ccos, csin = rope
    tok = lambda i, t: (i, t, 0)
    prev = lambda i, t: (i, jnp.maximum(t * (tm // hb) - 1, 0), 0)
    nxt = lambda i, t: (i, jnp.minimum((t + 1) * (tm // hb), l // hb - 1), 0)
    rtab = lambda i, t: (t, 0, 0)
    vec = lambda a: a.reshape(1, d)
    widths = (RET_QK_W, RET_QK_W, RET_V_W, RET_V_W, D_MODEL, D_MODEL, D_MODEL)
    return pl.pallas_call(
        _inproj_kernel,
        grid=(b, l // tm),
        in_specs=[pl.BlockSpec((1, tm, d), tok),
                  pl.BlockSpec((1, hb, d), prev),
                  pl.BlockSpec((1, hb, d), nxt),
                  _resident(w_in_bf.shape),
                  pl.BlockSpec((1, gpt, LANES), rtab),
                  pl.BlockSpec((1, gpt, LANES), rtab),
                  _resident(ccos.shape),
                  _resident(csin.shape),
                  _resident(conv_w.shape),
                  _resident((1, d)), _resident((1, d)), _resident((1, d))],
        out_specs=[pl.BlockSpec((1, tm, w), tok) for w in widths],
        out_shape=[jax.ShapeDtypeStruct((b, l, w), BF16) for w in widths],
        scratch_shapes=[pltpu.VMEM((tm + HALO, d), BF16),
                        pltpu.VMEM((tm + HALO, d), F32),
                        pltpu.VMEM((tm, d), F32)],
        compiler_params=_params(("parallel", "parallel")),
        name="inproj",
    )(u, u, u, w_in_bf, rcos.reshape(l // tm, gpt, LANES), rsin.reshape(l // tm, gpt, LANES),
      ccos, csin, conv_w, vec(conv_b), vec(ln_w), vec(ln_b))


def _rope_tables(l):
    half = RET_QK_DIM // 2
    inv = ROPE_BASE ** (-jnp.arange(0, half, 2, dtype=F32) / half)

    def tabs(pos):
        ang = pos[:, None] * inv[None, :]
        return (jnp.concatenate([jnp.cos(ang), jnp.cos(ang)], axis=-1),
                jnp.concatenate([-jnp.sin(ang), jnp.sin(ang)], axis=-1))

    rcos, rsin = tabs(jnp.arange(l // GRID_W, dtype=F32))
    ccos, csin = tabs(jnp.arange(GRID_W, dtype=F32))
    return rcos, rsin, ccos, csin


def _ret_kernel(lg_ref, q_ref, k_ref, v_ref, s0_ref, *rest, reverse):
    if reverse:
        o_ref, state = rest
    else:
        ob_ref, rg_ref, gnw_ref, o_ref, state = rest
    h = pl.program_id(1)

    @pl.when(pl.program_id(2) == 0)
    def _():
        state[...] = s0_ref[0, 0]

    c = RET_SUB
    lg = lg_ref[h]
    diff = (lax.broadcasted_iota(jnp.int32, (c, c), 0)
            - lax.broadcasted_iota(jnp.int32, (c, c), 1)).astype(F32)
    pos = lax.broadcasted_iota(jnp.int32, (c, 1), 0).astype(F32)
    if reverse:
        dmat = jnp.where(diff < 0, jnp.exp(lg * jnp.maximum(-diff, 0.0)), 0.0)
        qdec = jnp.exp(lg * (c - pos))
        kdec = jnp.exp(lg * pos)
    else:
        dmat = jnp.where(diff >= 0, jnp.exp(lg * jnp.maximum(diff, 0.0)), 0.0)
        qdec = jnp.exp(lg * (pos + 1.0))
        kdec = jnp.exp(lg * ((c - 1.0) - pos))
    cdec = jnp.exp(lg * jnp.full((1, 1), float(c), F32))

    nchunk = q_ref.shape[1] // c
    for ci in (reversed(range(nchunk)) if reverse else range(nchunk)):
        rows = slice(ci * c, (ci + 1) * c)
        qc = q_ref[0, rows, :]
        kc = k_ref[0, rows, :]
        vc = v_ref[0, rows, :]
        s = lax.dot_general(qc, kc, (((1,), (1,)), ((), ())), preferred_element_type=F32)
        inner = jnp.dot((s * dmat).astype(BF16), vc, preferred_element_type=F32)
        st = state[...]
        cross = jnp.dot(qc, st.astype(BF16), preferred_element_type=F32) * qdec
        o = inner + cross
        kd = (kc.astype(F32) * kdec).astype(BF16)
        upd = lax.dot_general(kd, vc, (((0,), (0,)), ((), ())), preferred_element_type=F32)
        state[...] = st * cdec + upd
        if reverse:
            o_ref[0, rows, :] = o.astype(BF16)
        else:
            o = o + ob_ref[0, rows, :].astype(F32)
            mu = jnp.mean(o, axis=-1, keepdims=True)
            dlt = o - mu
            var = jnp.mean(dlt * dlt, axis=-1, keepdims=True)
            on = (dlt * lax.rsqrt(var + EPS)) * gnw_ref[...]
            o_ref[0, rows, :] = (rg_ref[0, rows, :].astype(F32) * on).astype(BF16)


def _retention(lg, q, k, v, s0, extra, *, reverse, t):
    b, l, _ = q.shape
    nt = l // t
    tmap = (lambda i, h, j: (i, nt - 1 - j, h)) if reverse else (lambda i, h, j: (i, j, h))
    in_specs = [pl.BlockSpec(memory_space=pltpu.SMEM),
                pl.BlockSpec((1, t, RET_QK_DIM), tmap),
                pl.BlockSpec((1, t, RET_QK_DIM), tmap),
                pl.BlockSpec((1, t, RET_V_DIM), tmap),
                pl.BlockSpec((1, 1, RET_QK_DIM, RET_V_DIM), lambda i, h, j: (i, h, 0, 0))]
    if not reverse:
        in_specs += [pl.BlockSpec((1, t, RET_V_DIM), tmap),
                     pl.BlockSpec((1, t, RET_V_DIM), tmap),
                     pl.BlockSpec((1, RET_V_DIM), lambda i, h, j: (0, h))]
    return pl.pallas_call(
        functools.partial(_ret_kernel, reverse=reverse),
        grid=(b, RET_HEADS, nt),
        in_specs=in_specs,
        out_specs=pl.BlockSpec((1, t, RET_V_DIM), tmap),
        out_shape=jax.ShapeDtypeStruct((b, l, RET_V_W), BF16),
        scratch_shapes=[pltpu.VMEM((RET_QK_DIM, RET_V_DIM), F32)],
        compiler_params=_params(("parallel", "parallel", "arbitrary")),
        name="ret_bwd" if reverse else "ret_fwd",
    )(lg, q, k, v, s0, *extra)


def _mixer_kernel(yrp_ref, ys_ref, ga_ref, gb_ref, h_ref, mods_ref, wro_ref, wco_ref, wo_ref, o_ref):
    y_conv = jnp.dot(ys_ref[0], wco_ref[...], preferred_element_type=F32)
    y_ret = jnp.dot(yrp_ref[0], wro_ref[...], preferred_element_type=F32)
    merged = (ga_ref[0].astype(F32) * y_ret + gb_ref[0].astype(F32) * y_conv).astype(BF16)
    m = jnp.dot(merged, wo_ref[...], preferred_element_type=F32)
    g2 = mods_ref[0, 5:6, :]
    o_ref[0] = h_ref[0] + g2 * m


def _mixer(yrp, ys, sga, sgb, h, mods, wro, wco, wo, *, tm):
    b, l, d = h.shape
    tok = lambda i, t: (i, t, 0)
    return pl.pallas_call(
        _mixer_kernel,
        grid=(b, l // tm),
        in_specs=[pl.BlockSpec((1, tm, RET_V_W), tok),
                  pl.BlockSpec((1, tm, d), tok),
                  pl.BlockSpec((1, tm, d), tok),
                  pl.BlockSpec((1, tm, d), tok),
                  pl.BlockSpec((1, tm, d), tok),
                  pl.BlockSpec((1, N_MOD, d), lambda i, t: (i, 0, 0)),
                  _resident(wro.shape),
                  _resident(wco.shape),
                  _resident(wo.shape)],
        out_specs=pl.BlockSpec((1, tm, d), tok),
        out_shape=jax.ShapeDtypeStruct((b, l, d), F32),
        compiler_params=_params(("parallel", "parallel")),
        name="mixer",
    )(yrp, ys, sga, sgb, h, mods, wro, wco, wo)


def kernel(x, c, ctx, c_ctx, w_mod, b_mod, norm_ffn1, w_ffn1_in, w_ffn1_out, norm_mix, w_in, ret_decay_f, ret_decay_b, ret_gn_w, w_ret_out, conv_w, conv_b, conv_ln_w, conv_ln_b, w_conv_out, w_out, norm_ffn2, w_ffn2_in, w_ffn2_out, final_norm):
    assert w_mod.shape[0] == 1, "single-layer block"
    b, l, d = x.shape
    assert l % max(FFN_TM, RET_T, MIX_TM) == 0 and d == D_MODEL

    cc = jnp.zeros((SUBLANES, d), F32).at[:b].set(c).at[b].set(c_ctx)
    mods_all = _mods(cc, w_mod[0], b_mod[0])
    mods = mods_all[:b].reshape(b, N_MOD, d)
    mods_c = mods_all[b:b + 1].reshape(1, N_MOD, d)

    col = jnp.arange(IN_WIDTH)
    kscale = jnp.where((col >= K0) & (col < V0), RET_QK_DIM ** -0.5, 1.0).astype(F32)
    w_in_bf = (w_in[0] * kscale[None, :]).astype(BF16)
    w1a, w1b = w_ffn1_in[0].astype(BF16), w_ffn1_out[0].astype(BF16)
    w2a, w2b = w_ffn2_in[0].astype(BF16), w_ffn2_out[0].astype(BF16)
    lgf = jax.nn.log_sigmoid(ret_decay_f[0].astype(F32))
    lgb = jax.nn.log_sigmoid(ret_decay_b[0].astype(F32))

    h, u = _ffn(x, mods, norm_ffn1[0], w1a, w1b, norm_mix[0], row0=0, final=False, tm=FFN_TM)
    _, uc = _ffn(ctx, mods_c, norm_ffn1[0], w1a, w1b, norm_mix[0], row0=0, final=False, tm=FFN_TM)

    rf, rb = _ctx_state(uc, w_in_bf, lgf, lgb)

    q, k, v, srg, ys, sga, sgb = _inproj(u, w_in_bf, _rope_tables(l), conv_w[0], conv_b[0],
                                         conv_ln_w[0], conv_ln_b[0], tm=INPROJ_TM)

    o_b = _retention(lgb, q, k, v, rb, (), reverse=True, t=RET_T)
    yrp = _retention(lgf, q, k, v, rf, (o_b, srg, ret_gn_w[0].reshape(1, RET_V_W)), reverse=False, t=RET_T)

    h2 = _mixer(yrp, ys, sga, sgb, h, mods, w_ret_out[0].astype(BF16), w_conv_out[0].astype(BF16),
                w_out[0].astype(BF16), tm=MIX_TM)

    return _ffn(h2, mods, norm_ffn2[0], w2a, w2b, final_norm, row0=6, final=True, tm=FFN_TM)
```

```python
import functools

import jax
import jax.numpy as jnp
from jax import lax
from jax.experimental import pallas as pl
from jax.experimental.pallas import tpu as pltpu

F32 = jnp.float32
BF16 = jnp.bfloat16

D_MODEL = 1024
GRID_W = 64
RET_HEADS = 4
RET_QK_DIM = 256
RET_V_DIM = 512
CONV_WIDTH = 31
CONV_PAD = CONV_WIDTH // 2
D_FF = 2816
ROPE_BASE = 10000.0
EPS = 1e-6
N_MOD = 9

RET_QK_W = RET_HEADS * RET_QK_DIM
RET_V_W = RET_HEADS * RET_V_DIM
K0 = RET_QK_W
V0 = 2 * RET_QK_W
G0 = V0 + RET_V_W
C0 = G0 + RET_V_W
GA0 = C0 + 2 * D_MODEL
GB0 = GA0 + D_MODEL
IN_WIDTH = GB0 + D_MODEL

V7X_VMEM_LIMIT_BYTES = 56 * 1024 * 1024
SUBLANES = 8
LANES = 128
BF16_ROWS = 16
HALO = 2 * BF16_ROWS

FFN_CHUNK = 256
FFN_TM = 512
INPROJ_TM = 256
INPROJ_TN = 256
MIX_TM = 512
RET_T = 2048
RET_SUB = 256
CONV_RB = 64


def _resident(shape):
    nd = len(shape)
    return pl.BlockSpec(shape, lambda *_: (0,) * nd, pipeline_mode=pl.Buffered(1))


def _params(sem):
    return pltpu.CompilerParams(dimension_semantics=sem, vmem_limit_bytes=V7X_VMEM_LIMIT_BYTES)


def _rms_mod(x, w, scale, shift):
    y = x * lax.rsqrt(jnp.mean(x * x, axis=-1, keepdims=True) + EPS)
    return (y * w) * (1.0 + scale) + shift


def _layernorm(x, w, b):
    mu = jnp.mean(x, axis=-1, keepdims=True)
    d = x - mu
    var = jnp.mean(d * d, axis=-1, keepdims=True)
    return (d * lax.rsqrt(var + EPS)) * w + b


def _mods_kernel(c_ref, w_ref, b_ref, o_ref):
    c = c_ref[...]
    s = c * jax.nn.sigmoid(c)
    o_ref[...] = jnp.dot(s, w_ref[...], preferred_element_type=F32,
                         precision=lax.Precision.HIGHEST) + b_ref[...]


def _mods(cc, w_mod, b_mod):
    rows, d = cc.shape
    n = w_mod.shape[1]
    tn = D_MODEL
    return pl.pallas_call(
        _mods_kernel,
        grid=(n // tn,),
        in_specs=[pl.BlockSpec((rows, d), lambda j: (0, 0)),
                  pl.BlockSpec((d, tn), lambda j: (0, j)),
                  pl.BlockSpec((1, tn), lambda j: (0, j))],
        out_specs=pl.BlockSpec((rows, tn), lambda j: (0, j)),
        out_shape=jax.ShapeDtypeStruct((rows, n), F32),
        compiler_params=_params(("arbitrary",)),
        name="mods",
    )(cc, w_mod, b_mod.reshape(1, n))


def _ffn_kernel(x_ref, mods_ref, nw_ref, w1_ref, w2_ref, n2w_ref, *out_refs, row0, final):
    x = x_ref[0]
    sh = mods_ref[0, row0:row0 + 1, :]
    sc = mods_ref[0, row0 + 1:row0 + 2, :]
    g = mods_ref[0, row0 + 2:row0 + 3, :]
    xm = _rms_mod(x, nw_ref[...], sc, sh).astype(BF16)
    ck = FFN_CHUNK
    acc = jnp.zeros_like(x)
    for c0 in range(0, D_FF, ck):
        a = jnp.dot(xm, w1_ref[:, c0:c0 + ck], preferred_element_type=F32)
        b = jnp.dot(xm, w1_ref[:, D_FF + c0:D_FF + c0 + ck], preferred_element_type=F32)
        act = (a * jax.nn.sigmoid(a) * b).astype(BF16)
        acc = acc + jnp.dot(act, w2_ref[c0:c0 + ck, :], preferred_element_type=F32)
    h = x + (0.5 * g) * acc
    if final:
        (o_ref,) = out_refs
        y = h * lax.rsqrt(jnp.mean(h * h, axis=-1, keepdims=True) + EPS)
        o_ref[0] = y * n2w_ref[...]
    else:
        h_ref, u_ref = out_refs
        h_ref[0] = h
        sh2 = mods_ref[0, row0 + 3:row0 + 4, :]
        sc2 = mods_ref[0, row0 + 4:row0 + 5, :]
        u_ref[0] = _rms_mod(h, n2w_ref[...], sc2, sh2).astype(BF16)


def _ffn(x, mods, norm_w, w1, w2, norm2_w, *, row0, final, tm):
    b, l, d = x.shape
    tm = min(tm, l)
    mods_map = (lambda i, t: (i, 0, 0)) if mods.shape[0] == b else (lambda i, t: (0, 0, 0))
    tok = lambda i, t: (i, t, 0)
    in_specs = [pl.BlockSpec((1, tm, d), tok),
                pl.BlockSpec((1, N_MOD, d), mods_map),
                _resident((1, d)),
                _resident(w1.shape),
                _resident(w2.shape),
                _resident((1, d))]
    if final:
        out_specs = pl.BlockSpec((1, tm, d), tok)
        out_shape = jax.ShapeDtypeStruct((b, l, d), F32)
    else:
        out_specs = [pl.BlockSpec((1, tm, d), tok), pl.BlockSpec((1, tm, d), tok)]
        out_shape = [jax.ShapeDtypeStruct((b, l, d), F32), jax.ShapeDtypeStruct((b, l, d), BF16)]
    return pl.pallas_call(
        functools.partial(_ffn_kernel, row0=row0, final=final),
        grid=(b, l // tm),
        in_specs=in_specs,
        out_specs=out_specs,
        out_shape=out_shape,
        compiler_params=_params(("parallel", "parallel")),
        name="ffn_final" if final else "ffn_mix",
    )(x, mods, norm_w.reshape(1, d), w1, w2, norm2_w.reshape(1, d))


def _ctx_state_kernel(lgf_ref, lgb_ref, u_ref, wk_ref, wv_ref, rf_ref, rb_ref):
    h = pl.program_id(1)
    u = u_ref[0]
    n = u.shape[0]
    k = jnp.dot(u, wk_ref[...], preferred_element_type=F32)
    v = jnp.dot(u, wv_ref[...], preferred_element_type=F32).astype(BF16)
    pos = lax.broadcasted_iota(jnp.int32, (n, 1), 0).astype(F32)
    wf = jnp.exp(lgf_ref[h] * ((n - 1.0) - pos))
    wb = jnp.exp(lgb_ref[h] * pos)
    tdot = lambda a, c: lax.dot_general(a, c, (((0,), (0,)), ((), ())), preferred_element_type=F32)
    rf_ref[0, 0] = tdot((k * wf).astype(BF16), v)
    rb_ref[0, 0] = tdot((k * wb).astype(BF16), v)


def _ctx_state(uc, w_in_bf, lgf, lgb):
    b, n, d = uc.shape
    smem = pl.BlockSpec(memory_space=pltpu.SMEM)
    st = jax.ShapeDtypeStruct((b, RET_HEADS, RET_QK_DIM, RET_V_DIM), F32)
    st_spec = pl.BlockSpec((1, 1, RET_QK_DIM, RET_V_DIM), lambda i, h: (i, h, 0, 0))
    return pl.pallas_call(
        _ctx_state_kernel,
        grid=(b, RET_HEADS),
        in_specs=[smem, smem,
                  pl.BlockSpec((1, n, d), lambda i, h: (i, 0, 0)),
                  pl.BlockSpec((d, RET_QK_DIM), lambda i, h: (0, K0 // RET_QK_DIM + h)),
                  pl.BlockSpec((d, RET_V_DIM), lambda i, h: (0, V0 // RET_V_DIM + h))],
        out_specs=[st_spec, st_spec],
        out_shape=[st, st],
        compiler_params=_params(("parallel", "parallel")),
        name="ctx_state",
    )(lgf, lgb, uc, w_in_bf, w_in_bf)


def _depthwise_conv(ybuf, cw_ref, cbuf, c0, r0, zero):
    rows = CONV_RB + HALO
    base = BF16_ROWS - CONV_PAD
    cs = slice(c0, c0 + LANES)
    src = ybuf[r0:r0 + rows, cs]
    parts = [None] * (HALO // SUBLANES)
    for r in range(SUBLANES):
        zr = src if r == 0 else pltpu.roll(src, rows - r, 0)
        for ai, a0 in enumerate(range(0, HALO, SUBLANES)):
            j = a0 + r - base
            if 0 <= j < CONV_WIDTH:
                term = zr[a0:a0 + CONV_RB, :] * cw_ref[j:j + 1, cs]
                parts[ai] = term if parts[ai] is None else parts[ai] + term
    acc = (parts[0] + parts[1]) + (parts[2] + parts[3])
    cbuf[r0:r0 + CONV_RB, cs] = acc + jnp.tile(zero, (CONV_RB // SUBLANES, 1))


def _inproj_kernel(u_ref, uprev_ref, unext_ref, w_ref, rcos_ref, rsin_ref, ccos_ref, csin_ref,
                   cw_ref, cb_ref, lnw_ref, lnb_ref,
                   q_ref, k_ref, v_ref, rg_ref, ys_ref, ga_ref, gb_ref, uext, ybuf, cbuf):
    t = pl.program_id(1)
    tm = u_ref.shape[1]
    hb = BF16_ROWS
    u = u_ref[0]

    def proj(c0):
        return jnp.dot(u, w_ref[:, c0:c0 + INPROJ_TN], preferred_element_type=F32)

    def rope(val, ref, o0):
        for gi in range(tm // GRID_W):
            rs = slice(gi * GRID_W, (gi + 1) * GRID_W)
            for l0 in range(0, INPROJ_TN, LANES):
                if (o0 + l0) % RET_QK_DIM == 0:
                    cs, sn = rcos_ref[0, gi:gi + 1, :], rsin_ref[0, gi:gi + 1, :]
                else:
                    cs, sn = ccos_ref[...], csin_ref[...]
                slab = val[rs, l0:l0 + LANES]
                ref[0, rs, o0 + l0:o0 + l0 + LANES] = (
                    slab * cs + pltpu.roll(slab, LANES // 2, 1) * sn).astype(BF16)

    def piece(c0):
        val = proj(c0)
        if c0 < K0:
            rope(val, q_ref, c0)
        elif c0 < V0:
            rope(val, k_ref, c0 - K0)
        elif c0 < G0:
            v_ref[0, :, c0 - V0:c0 - V0 + INPROJ_TN] = val.astype(BF16)
        elif c0 < C0:
            rg_ref[0, :, c0 - G0:c0 - G0 + INPROJ_TN] = (val * jax.nn.sigmoid(val)).astype(BF16)
        elif c0 < GB0:
            ga_ref[0, :, c0 - GA0:c0 - GA0 + INPROJ_TN] = jax.nn.sigmoid(val).astype(BF16)
        else:
            gb_ref[0, :, c0 - GB0:c0 - GB0 + INPROJ_TN] = jax.nn.sigmoid(val).astype(BF16)
        return val

    uext[0:hb, :] = jnp.where(t > 0, uprev_ref[0], jnp.zeros_like(uprev_ref[0]))
    uext[hb:hb + tm, :] = u
    uext[hb + tm:, :] = jnp.where(t < pl.num_programs(1) - 1, unext_ref[0], jnp.zeros_like(unext_ref[0]))
    ue = uext[...]
    ca = jnp.dot(ue, w_ref[:, C0:C0 + D_MODEL], preferred_element_type=F32)
    cg = jnp.dot(ue, w_ref[:, C0 + D_MODEL:GA0], preferred_element_type=F32)
    ybuf[...] = ca * jax.nn.sigmoid(cg)

    cols = [c0 for c0 in range(0, IN_WIDTH, INPROJ_TN) if not C0 <= c0 < GA0]
    blocks = [(c0, r0) for c0 in range(0, D_MODEL, LANES) for r0 in range(0, tm, CONV_RB)]
    assert len(cols) == len(blocks)
    zero = jnp.zeros((SUBLANES, LANES), F32)
    for c0, (b0, r0) in zip(cols, blocks):
        _depthwise_conv(ybuf, cw_ref, cbuf, b0, r0, zero)
        bits = piece(c0)[0:SUBLANES, 0:LANES].astype(jnp.int32)
        zero = lax.shift_right_logical(lax.shift_right_logical(bits, 16), 16).astype(F32)

    yn = _layernorm(cbuf[...] + cb_ref[...], lnw_ref[...], lnb_ref[...])
    ys_ref[0] = (yn * jax.nn.sigmoid(yn)).astype(BF16)


def _inproj(u, w_in_bf, rope, conv_w, conv_b, ln_w, ln_b, *, tm):
    b, l, d = u.shape
    hb = BF16_ROWS
    gpt = tm // GRID_W
    rcos, rsin, ccos, csin = rope
    tok = lambda i, t: (i, t, 0)
    prev = lambda i, t: (i, jnp.maximum(t * (tm // hb) - 1, 0), 0)
    nxt = lambda i, t: (i, jnp.minimum((t + 1) * (tm // hb), l // hb - 1), 0)
    rtab = lambda i, t: (t, 0, 0)
    vec = lambda a: a.reshape(1, d)
    widths = (RET_QK_W, RET_QK_W, RET_V_W, RET_V_W, D_MODEL, D_MODEL, D_MODEL)
    return pl.pallas_call(
        _inproj_kernel,
        grid=(b, l // tm),
        in_specs=[pl.BlockSpec((1, tm, d), tok),
                  pl.BlockSpec((1, hb, d), prev),
                  pl.BlockSpec((1, hb, d), nxt),
                  _resident(w_in_bf.shape),
                  pl.BlockSpec((1, gpt, LANES), rtab),
                  pl.BlockSpec((1, gpt, LANES), rtab),
                  _resident(ccos.shape),
                  _resident(csin.shape),
                  _resident(conv_w.shape),
                  _resident((1, d)), _resident((1, d)), _resident((1, d))],
        out_specs=[pl.BlockSpec((1, tm, w), tok) for w in widths],
        out_shape=[jax.ShapeDtypeStruct((b, l, w), BF16) for w in widths],
        scratch_shapes=[pltpu.VMEM((tm + HALO, d), BF16),
                        pltpu.VMEM((tm + HALO, d), F32),
                        pltpu.VMEM((tm, d), F32)],
        compiler_params=_params(("parallel", "parallel")),
        name="inproj",
    )(u, u, u, w_in_bf, rcos.reshape(l // tm, gpt, LANES), rsin.reshape(l // tm, gpt, LANES),
      ccos, csin, conv_w, vec(conv_b), vec(ln_w), vec(ln_b))


def _rope_tables(l):
    half = RET_QK_DIM // 2
    inv = ROPE_BASE ** (-jnp.arange(0, half, 2, dtype=F32) / half)

    def tabs(pos):
        ang = pos[:, None] * inv[None, :]
        return (jnp.concatenate([jnp.cos(ang), jnp.cos(ang)], axis=-1),
                jnp.concatenate([-jnp.sin(ang), jnp.sin(ang)], axis=-1))

    rcos, rsin = tabs(jnp.arange(l // GRID_W, dtype=F32))
    ccos, csin = tabs(jnp.arange(GRID_W, dtype=F32))
    return rcos, rsin, ccos, csin


def _ret_kernel(lg_ref, q_ref, k_ref, v_ref, s0_ref, *rest, reverse):
    if reverse:
        o_ref, state, ibuf, ubuf = rest
    else:
        ob_ref, rg_ref, gnw_ref, o_ref, state, ibuf, ubuf = rest
    h = pl.program_id(1)

    @pl.when(pl.program_id(2) == 0)
    def _():
        state[...] = s0_ref[0, 0]

    c = RET_SUB
    lg = lg_ref[h]
    diff = (lax.broadcasted_iota(jnp.int32, (c, c), 0)
            - lax.broadcasted_iota(jnp.int32, (c, c), 1)).astype(F32)
    pos = lax.broadcasted_iota(jnp.int32, (c, 1), 0).astype(F32)
    if reverse:
        dmat = jnp.where(diff < 0, jnp.exp(lg * jnp.maximum(-diff, 0.0)), 0.0)
        qdec = jnp.exp(lg * (c - pos))
        kdec = jnp.exp(lg * pos)
    else:
        dmat = jnp.where(diff >= 0, jnp.exp(lg * jnp.maximum(diff, 0.0)), 0.0)
        qdec = jnp.exp(lg * (pos + 1.0))
        kdec = jnp.exp(lg * ((c - 1.0) - pos))
    cdec = jnp.exp(lg * jnp.full((1, 1), float(c), F32))

    nchunk = q_ref.shape[1] // c
    order = list(reversed(range(nchunk))) if reverse else list(range(nchunk))
    def local_terms(ci):
        rows = slice(ci * c, (ci + 1) * c)
        qc = q_ref[0, rows, :]
        kc = k_ref[0, rows, :]
        vc = v_ref[0, rows, :]
        s = lax.dot_general(qc, kc, (((1,), (1,)), ((), ())), preferred_element_type=F32)
        ibuf[rows, :] = jnp.dot((s * dmat).astype(BF16), vc, preferred_element_type=F32)
        kd = (kc.astype(F32) * kdec).astype(BF16)
        ubuf[ci] = lax.dot_general(kd, vc, (((0,), (0,)), ((), ())), preferred_element_type=F32)

    local_terms(order[0])
    st = state[...]
    for n, ci in enumerate(order):
        if n + 1 < nchunk:
            local_terms(order[n + 1])
        rows = slice(ci * c, (ci + 1) * c)
        cross = jnp.dot(q_ref[0, rows, :], st.astype(BF16), preferred_element_type=F32) * qdec
        o = ibuf[rows, :] + cross
        st = st * cdec + ubuf[ci]
        if ci == order[-1]:
            state[...] = st
        if reverse:
            o_ref[0, rows, :] = o.astype(BF16)
        else:
            o = o + ob_ref[0, rows, :].astype(F32)
            mu = jnp.mean(o, axis=-1, keepdims=True)
            dlt = o - mu
            var = jnp.mean(dlt * dlt, axis=-1, keepdims=True)
            on = (dlt * lax.rsqrt(var + EPS)) * gnw_ref[...]
            o_ref[0, rows, :] = (rg_ref[0, rows, :].astype(F32) * on).astype(BF16)


def _retention(lg, q, k, v, s0, extra, *, reverse, t):
    b, l, _ = q.shape
    nt = l // t
    tmap = (lambda i, h, j: (i, nt - 1 - j, h)) if reverse else (lambda i, h, j: (i, j, h))
    in_specs = [pl.BlockSpec(memory_space=pltpu.SMEM),
                pl.BlockSpec((1, t, RET_QK_DIM), tmap),
                pl.BlockSpec((1, t, RET_QK_DIM), tmap),
                pl.BlockSpec((1, t, RET_V_DIM), tmap),
                pl.BlockSpec((1, 1, RET_QK_DIM, RET_V_DIM), lambda i, h, j: (i, h, 0, 0))]
    if not reverse:
        in_specs += [pl.BlockSpec((1, t, RET_V_DIM), tmap),
                     pl.BlockSpec((1, t, RET_V_DIM), tmap),
                     pl.BlockSpec((1, RET_V_DIM), lambda i, h, j: (0, h))]
    return pl.pallas_call(
        functools.partial(_ret_kernel, reverse=reverse),
        grid=(b, RET_HEADS, nt),
        in_specs=in_specs,
        out_specs=pl.BlockSpec((1, t, RET_V_DIM), tmap),
        out_shape=jax.ShapeDtypeStruct((b, l, RET_V_W), BF16),
        scratch_shapes=[pltpu.VMEM((RET_QK_DIM, RET_V_DIM), F32),
                        pltpu.VMEM((t, RET_V_DIM), F32),
                        pltpu.VMEM((t // RET_SUB, RET_QK_DIM, RET_V_DIM), F32)],
        compiler_params=_params(("parallel", "parallel", "arbitrary")),
        name="ret_bwd" if reverse else "ret_fwd",
    )(lg, q, k, v, s0, *extra)


def _mixer_kernel(yrp_ref, ys_ref, ga_ref, gb_ref, h_ref, mods_ref, wro_ref, wco_ref, wo_ref, o_ref):
    y_conv = jnp.dot(ys_ref[0], wco_ref[...], preferred_element_type=F32)
    y_ret = jnp.dot(yrp_ref[0], wro_ref[...], preferred_element_type=F32)
    merged = (ga_ref[0].astype(F32) * y_ret + gb_ref[0].astype(F32) * y_conv).astype(BF16)
    m = jnp.dot(merged, wo_ref[...], preferred_element_type=F32)
    g2 = mods_ref[0, 5:6, :]
    o_ref[0] = h_ref[0] + g2 * m


def _mixer(yrp, ys, sga, sgb, h, mods, wro, wco, wo, *, tm):
    b, l, d = h.shape
    tok = lambda i, t: (i, t, 0)
    return pl.pallas_call(
        _mixer_kernel,
        grid=(b, l // tm),
        in_specs=[pl.BlockSpec((1, tm, RET_V_W), tok),
                  pl.BlockSpec((1, tm, d), tok),
                  pl.BlockSpec((1, tm, d), tok),
                  pl.BlockSpec((1, tm, d), tok),
                  pl.BlockSpec((1, tm, d), tok),
                  pl.BlockSpec((1, N_MOD, d), lambda i, t: (i, 0, 0)),
                  _resident(wro.shape),
                  _resident(wco.shape),
                  _resident(wo.shape)],
        out_specs=pl.BlockSpec((1, tm, d), tok),
        out_shape=jax.ShapeDtypeStruct((b, l, d), F32),
        compiler_params=_params(("parallel", "parallel")),
        name="mixer",
    )(yrp, ys, sga, sgb, h, mods, wro, wco, wo)


def kernel(x, c, ctx, c_ctx, w_mod, b_mod, norm_ffn1, w_ffn1_in, w_ffn1_out, norm_mix, w_in, ret_decay_f, ret_decay_b, ret_gn_w, w_ret_out, conv_w, conv_b, conv_ln_w, conv_ln_b, w_conv_out, w_out, norm_ffn2, w_ffn2_in, w_ffn2_out, final_norm):
    assert w_mod.shape[0] == 1, "single-layer block"
    b, l, d = x.shape
    assert l % max(FFN_TM, RET_T, MIX_TM) == 0 and d == D_MODEL

    cc = jnp.zeros((SUBLANES, d), F32).at[:b].set(c).at[b].set(c_ctx)
    mods_all = _mods(cc, w_mod[0], b_mod[0])
    mods = mods_all[:b].reshape(b, N_MOD, d)
    mods_c = mods_all[b:b + 1].reshape(1, N_MOD, d)

    col = jnp.arange(IN_WIDTH)
    kscale = jnp.where((col >= K0) & (col < V0), RET_QK_DIM ** -0.5, 1.0).astype(F32)
    w_in_bf = (w_in[0] * kscale[None, :]).astype(BF16)
    w1a, w1b = w_ffn1_in[0].astype(BF16), w_ffn1_out[0].astype(BF16)
    w2a, w2b = w_ffn2_in[0].astype(BF16), w_ffn2_out[0].astype(BF16)
    lgf = jax.nn.log_sigmoid(ret_decay_f[0].astype(F32))
    lgb = jax.nn.log_sigmoid(ret_decay_b[0].astype(F32))

    h, u = _ffn(x, mods, norm_ffn1[0], w1a, w1b, norm_mix[0], row0=0, final=False, tm=FFN_TM)
    _, uc = _ffn(ctx, mods_c, norm_ffn1[0], w1a, w1b, norm_mix[0], row0=0, final=False, tm=FFN_TM)

    rf, rb = _ctx_state(uc, w_in_bf, lgf, lgb)

    q, k, v, srg, ys, sga, sgb = _inproj(u, w_in_bf, _rope_tables(l), conv_w[0], conv_b[0],
                                         conv_ln_w[0], conv_ln_b[0], tm=INPROJ_TM)

    o_b = _retention(lgb, q, k, v, rb, (), reverse=True, t=RET_T)
    yrp = _retention(lgf, q, k, v, rf, (o_b, srg, ret_gn_w[0].reshape(1, RET_V_W)), reverse=False, t=RET_T)

    h2 = _mixer(yrp, ys, sga, sgb, h, mods, w_ret_out[0].astype(BF16), w_conv_out[0].astype(BF16),
                w_out[0].astype(BF16), tm=MIX_TM)

    return _ffn(h2, mods, norm_ffn2[0], w2a, w2b, final_norm, row0=6, final=True, tm=FFN_TM)
```

```python
import functools

import jax
import jax.numpy as jnp
from jax import lax
from jax.experimental import pallas as pl
from jax.experimental.pallas import tpu as pltpu

F32 = jnp.float32
BF16 = jnp.bfloat16

D_MODEL = 1024
GRID_W = 64
RET_HEADS = 4
RET_QK_DIM = 256
RET_V_DIM = 512
CONV_WIDTH = 31
CONV_PAD = CONV_WIDTH // 2
D_FF = 2816
ROPE_BASE = 10000.0
EPS = 1e-6
N_MOD = 9

RET_QK_W = RET_HEADS * RET_QK_DIM
RET_V_W = RET_HEADS * RET_V_DIM
K0 = RET_QK_W
V0 = 2 * RET_QK_W
G0 = V0 + RET_V_W
C0 = G0 + RET_V_W
GA0 = C0 + 2 * D_MODEL
GB0 = GA0 + D_MODEL
IN_WIDTH = GB0 + D_MODEL

V7X_VMEM_LIMIT_BYTES = 56 * 1024 * 1024
SUBLANES = 8
LANES = 128
BF16_ROWS = 16
HALO = 2 * BF16_ROWS

MODS_TN = 2304
FFN_CHUNK = 256
FFN_TM = 512
INPROJ_TM = 256
INPROJ_TN = 256
MIX_TM = 512
RET_T = 2048
RET_SUB = 256
CONV_RB = 64


def _resident(shape):
    nd = len(shape)
    return pl.BlockSpec(shape, lambda *_: (0,) * nd, pipeline_mode=pl.Buffered(1))


def _params(sem):
    return pltpu.CompilerParams(dimension_semantics=sem, vmem_limit_bytes=V7X_VMEM_LIMIT_BYTES)


def _rms_mod(x, w, scale, shift):
    y = x * lax.rsqrt(jnp.mean(x * x, axis=-1, keepdims=True) + EPS)
    return (y * w) * (1.0 + scale) + shift


def _layernorm(x, w, b):
    mu = jnp.mean(x, axis=-1, keepdims=True)
    d = x - mu
    var = jnp.mean(d * d, axis=-1, keepdims=True)
    return (d * lax.rsqrt(var + EPS)) * w + b


def _mods_kernel(c_ref, w_ref, b_ref, o_ref):
    c = c_ref[...]
    s = c * jax.nn.sigmoid(c)
    o_ref[...] = jnp.dot(s, w_ref[...], preferred_element_type=F32,
                         precision=lax.Precision.HIGHEST) + b_ref[...]


def _mods(cc, w_mod, b_mod):
    rows, d = cc.shape
    n = w_mod.shape[1]
    tn = MODS_TN
    return pl.pallas_call(
        _mods_kernel,
        grid=(n // tn,),
        in_specs=[pl.BlockSpec((rows, d), lambda j: (0, 0)),
                  pl.BlockSpec((d, tn), lambda j: (0, j)),
                  pl.BlockSpec((1, tn), lambda j: (0, j))],
        out_specs=pl.BlockSpec((rows, tn), lambda j: (0, j)),
        out_shape=jax.ShapeDtypeStruct((rows, n), F32),
        compiler_params=_params(("arbitrary",)),
        name="mods",
    )(cc, w_mod, b_mod.reshape(1, n))


def _ffn_kernel(x_ref, mods_ref, nw_ref, w1_ref, w2_ref, n2w_ref, *out_refs, row0, final):
    x = x_ref[0]
    sh = mods_ref[0, row0:row0 + 1, :]
    sc = mods_ref[0, row0 + 1:row0 + 2, :]
    g = mods_ref[0, row0 + 2:row0 + 3, :]
    xm = _rms_mod(x, nw_ref[...], sc, sh).astype(BF16)
    ck = FFN_CHUNK
    acc = jnp.zeros_like(x)
    for c0 in range(0, D_FF, ck):
        a = jnp.dot(xm, w1_ref[:, c0:c0 + ck], preferred_element_type=F32)
        b = jnp.dot(xm, w1_ref[:, D_FF + c0:D_FF + c0 + ck], preferred_element_type=F32)
        act = (a * jax.nn.sigmoid(a) * b).astype(BF16)
        acc = acc + jnp.dot(act, w2_ref[c0:c0 + ck, :], preferred_element_type=F32)
    h = x + (0.5 * g) * acc
    if final:
        (o_ref,) = out_refs
        y = h * lax.rsqrt(jnp.mean(h * h, axis=-1, keepdims=True) + EPS)
        o_ref[0] = y * n2w_ref[...]
    else:
        h_ref, u_ref = out_refs
        h_ref[0] = h
        sh2 = mods_ref[0, row0 + 3:row0 + 4, :]
        sc2 = mods_ref[0, row0 + 4:row0 + 5, :]
        u_ref[0] = _rms_mod(h, n2w_ref[...], sc2, sh2).astype(BF16)


def _ffn(x, mods, norm_w, w1, w2, norm2_w, *, row0, final, tm):
    b, l, d = x.shape
    tm = min(tm, l)
    mods_map = (lambda i, t: (i, 0, 0)) if mods.shape[0] == b else (lambda i, t: (0, 0, 0))
    tok = lambda i, t: (i, t, 0)
    in_specs = [pl.BlockSpec((1, tm, d), tok),
                pl.BlockSpec((1, N_MOD, d), mods_map),
                _resident((1, d)),
                _resident(w1.shape),
                _resident(w2.shape),
                _resident((1, d))]
    if final:
        out_specs = pl.BlockSpec((1, tm, d), tok)
        out_shape = jax.ShapeDtypeStruct((b, l, d), F32)
    else:
        out_specs = [pl.BlockSpec((1, tm, d), tok), pl.BlockSpec((1, tm, d), tok)]
        out_shape = [jax.ShapeDtypeStruct((b, l, d), F32), jax.ShapeDtypeStruct((b, l, d), BF16)]
    return pl.pallas_call(
        functools.partial(_ffn_kernel, row0=row0, final=final),
        grid=(b, l // tm),
        in_specs=in_specs,
        out_specs=out_specs,
        out_shape=out_shape,
        compiler_params=_params(("parallel", "parallel")),
        name="ffn_final" if final else "ffn_mix",
    )(x, mods, norm_w.reshape(1, d), w1, w2, norm2_w.reshape(1, d))


def _ctx_state_kernel(lgf_ref, lgb_ref, u_ref, wk_ref, wv_ref, rf_ref, rb_ref):
    h = pl.program_id(1)
    u = u_ref[0]
    n = u.shape[0]
    k = jnp.dot(u, wk_ref[...], preferred_element_type=F32)
    v = jnp.dot(u, wv_ref[...], preferred_element_type=F32).astype(BF16)
    pos = lax.broadcasted_iota(jnp.int32, (n, 1), 0).astype(F32)
    wf = jnp.exp(lgf_ref[h] * ((n - 1.0) - pos))
    wb = jnp.exp(lgb_ref[h] * pos)
    tdot = lambda a, c: lax.dot_general(a, c, (((0,), (0,)), ((), ())), preferred_element_type=F32)
    rf_ref[0, 0] = tdot((k * wf).astype(BF16), v)
    rb_ref[0, 0] = tdot((k * wb).astype(BF16), v)


def _ctx_state(uc, w_in_bf, lgf, lgb):
    b, n, d = uc.shape
    smem = pl.BlockSpec(memory_space=pltpu.SMEM)
    st = jax.ShapeDtypeStruct((b, RET_HEADS, RET_QK_DIM, RET_V_DIM), F32)
    st_spec = pl.BlockSpec((1, 1, RET_QK_DIM, RET_V_DIM), lambda i, h: (i, h, 0, 0))
    return pl.pallas_call(
        _ctx_state_kernel,
        grid=(b, RET_HEADS),
        in_specs=[smem, smem,
                  pl.BlockSpec((1, n, d), lambda i, h: (i, 0, 0)),
                  pl.BlockSpec((d, RET_QK_DIM), lambda i, h: (0, K0 // RET_QK_DIM + h)),
                  pl.BlockSpec((d, RET_V_DIM), lambda i, h: (0, V0 // RET_V_DIM + h))],
        out_specs=[st_spec, st_spec],
        out_shape=[st, st],
        compiler_params=_params(("parallel", "parallel")),
        name="ctx_state",
    )(lgf, lgb, uc, w_in_bf, w_in_bf)


def _depthwise_conv(ybuf, cw_ref, cbuf, c0, r0, zero):
    rows = CONV_RB + HALO
    base = BF16_ROWS - CONV_PAD
    cs = slice(c0, c0 + LANES)
    src = ybuf[r0:r0 + rows, cs]
    parts = [None] * (HALO // SUBLANES)
    for r in range(SUBLANES):
        zr = src if r == 0 else pltpu.roll(src, rows - r, 0)
        for ai, a0 in enumerate(range(0, HALO, SUBLANES)):
            j = a0 + r - base
            if 0 <= j < CONV_WIDTH:
                term = zr[a0:a0 + CONV_RB, :] * cw_ref[j:j + 1, cs]
                parts[ai] = term if parts[ai] is None else parts[ai] + term
    acc = (parts[0] + parts[1]) + (parts[2] + parts[3])
    cbuf[r0:r0 + CONV_RB, cs] = acc + jnp.tile(zero, (CONV_RB // SUBLANES, 1))


def _inproj_kernel(u_ref, uprev_ref, unext_ref, w_ref, rcos_ref, rsin_ref, ccos_ref, csin_ref,
                   cw_ref, cb_ref, lnw_ref, lnb_ref,
                   q_ref, k_ref, v_ref, rg_ref, ys_ref, ga_ref, gb_ref, uext, ybuf, cbuf):
    t = pl.program_id(1)
    tm = u_ref.shape[1]
    hb = BF16_ROWS
    u = u_ref[0]

    def proj(c0):
        return jnp.dot(u, w_ref[:, c0:c0 + INPROJ_TN], preferred_element_type=F32)

    def rope(val, ref, o0):
        for gi in range(tm // GRID_W):
            rs = slice(gi * GRID_W, (gi + 1) * GRID_W)
            for l0 in range(0, INPROJ_TN, LANES):
                if (o0 + l0) % RET_QK_DIM == 0:
                    cs, sn = rcos_ref[0, gi:gi + 1, :], rsin_ref[0, gi:gi + 1, :]
                else:
                    cs, sn = ccos_ref[...], csin_ref[...]
                slab = val[rs, l0:l0 + LANES]
                ref[0, rs, o0 + l0:o0 + l0 + LANES] = (
                    slab * cs + pltpu.roll(slab, LANES // 2, 1) * sn).astype(BF16)

    def piece(c0):
        val = proj(c0)
        if c0 < K0:
            rope(val, q_ref, c0)
        elif c0 < V0:
            rope(val, k_ref, c0 - K0)
        elif c0 < G0:
            v_ref[0, :, c0 - V0:c0 - V0 + INPROJ_TN] = val.astype(BF16)
        elif c0 < C0:
            rg_ref[0, :, c0 - G0:c0 - G0 + INPROJ_TN] = (val * jax.nn.sigmoid(val)).astype(BF16)
        elif c0 < GB0:
            ga_ref[0, :, c0 - GA0:c0 - GA0 + INPROJ_TN] = jax.nn.sigmoid(val).astype(BF16)
        else:
            gb_ref[0, :, c0 - GB0:c0 - GB0 + INPROJ_TN] = jax.nn.sigmoid(val).astype(BF16)
        return val

    uext[0:hb, :] = jnp.where(t > 0, uprev_ref[0], jnp.zeros_like(uprev_ref[0]))
    uext[hb:hb + tm, :] = u
    uext[hb + tm:, :] = jnp.where(t < pl.num_programs(1) - 1, unext_ref[0], jnp.zeros_like(unext_ref[0]))
    ue = uext[...]
    ca = jnp.dot(ue, w_ref[:, C0:C0 + D_MODEL], preferred_element_type=F32)
    cg = jnp.dot(ue, w_ref[:, C0 + D_MODEL:GA0], preferred_element_type=F32)
    ybuf[...] = ca * jax.nn.sigmoid(cg)

    cols = [c0 for c0 in range(0, IN_WIDTH, INPROJ_TN) if not C0 <= c0 < GA0]
    blocks = [(c0, r0) for c0 in range(0, D_MODEL, LANES) for r0 in range(0, tm, CONV_RB)]
    assert len(cols) == len(blocks)
    zero = jnp.zeros((SUBLANES, LANES), F32)
    for c0, (b0, r0) in zip(cols, blocks):
        _depthwise_conv(ybuf, cw_ref, cbuf, b0, r0, zero)
        bits = piece(c0)[0:SUBLANES, 0:LANES].astype(jnp.int32)
        zero = lax.shift_right_logical(lax.shift_right_logical(bits, 16), 16).astype(F32)

    yn = _layernorm(cbuf[...] + cb_ref[...], lnw_ref[...], lnb_ref[...])
    ys_ref[0] = (yn * jax.nn.sigmoid(yn)).astype(BF16)


def _inproj(u, w_in_bf, rope, conv_w, conv_b, ln_w, ln_b, *, tm):
    b, l, d = u.shape
    hb = BF16_ROWS
    gpt = tm // GRID_W
    rcos, rsin, ccos, csin = rope
    tok = lambda i, t: (i, t, 0)
    prev = lambda i, t: (i, jnp.maximum(t * (tm // hb) - 1, 0), 0)
    nxt = lambda i, t: (i, jnp.minimum((t + 1) * (tm // hb), l // hb - 1), 0)
    rtab = lambda i, t: (t, 0, 0)
    vec = lambda a: a.reshape(1, d)
    widths = (RET_QK_W, RET_QK_W, RET_V_W, RET_V_W, D_MODEL, D_MODEL, D_MODEL)
    return pl.pallas_call(
        _inproj_kernel,
        grid=(b, l // tm),
        in_specs=[pl.BlockSpec((1, tm, d), tok),
                  pl.BlockSpec((1, hb, d), prev),
                  pl.BlockSpec((1, hb, d), nxt),
                  _resident(w_in_bf.shape),
                  pl.BlockSpec((1, gpt, LANES), rtab),
                  pl.BlockSpec((1, gpt, LANES), rtab),
                  _resident(ccos.shape),
                  _resident(csin.shape),
                  _resident(conv_w.shape),
                  _resident((1, d)), _resident((1, d)), _resident((1, d))],
        out_specs=[pl.BlockSpec((1, tm, w), tok) for w in widths],
        out_shape=[jax.ShapeDtypeStruct((b, l, w), BF16) for w in widths],
        scratch_shapes=[pltpu.VMEM((tm + HALO, d), BF16),
                        pltpu.VMEM((tm + HALO, d), F32),
                        pltpu.VMEM((tm, d), F32)],
        compiler_params=_params(("parallel", "parallel")),
        name="inproj",
    )(u, u, u, w_in_bf, rcos.reshape(l // tm, gpt, LANES), rsin.reshape(l // tm, gpt, LANES),
      ccos, csin, conv_w, vec(conv_b), vec(ln_w), vec(ln_b))


def _rope_tables(l):
    half = RET_QK_DIM // 2
    inv = ROPE_BASE ** (-jnp.arange(0, half, 2, dtype=F32) / half)

    def tabs(pos):
        ang = pos[:, None] * inv[None, :]
        return (jnp.concatenate([jnp.cos(ang), jnp.cos(ang)], axis=-1),
                jnp.concatenate([-jnp.sin(ang), jnp.sin(ang)], axis=-1))

    rcos, rsin = tabs(jnp.arange(l // GRID_W, dtype=F32))
    ccos, csin = tabs(jnp.arange(GRID_W, dtype=F32))
    return rcos, rsin, ccos, csin


def _ret_kernel(*refs, reverse):
    if reverse:
        lg_ref, lgf_ref, q_ref, k_ref, v_ref, s0_ref, o_ref, state, ubuf, ibuf = refs
    else:
        lg_ref, q_ref, k_ref, v_ref, s0_ref, part_ref, gnw_ref, o_ref, state, ubuf = refs
    h = pl.program_id(1)

    @pl.when(pl.program_id(2) == 0)
    def _():
        state[...] = s0_ref[0, 0]

    c = RET_SUB
    lg = lg_ref[h]
    pos = lax.broadcasted_iota(jnp.int32, (c, 1), 0).astype(F32)
    if reverse:
        diff = (lax.broadcasted_iota(jnp.int32, (c, c), 0)
                - lax.broadcasted_iota(jnp.int32, (c, c), 1)).astype(F32)
        dmat = jnp.where(diff < 0, jnp.exp(lg * jnp.maximum(-diff, 0.0)),
                         jnp.exp(lgf_ref[h] * jnp.maximum(diff, 0.0)))
        qdec = jnp.exp(lg * (c - pos))
        kdec = jnp.exp(lg * pos)
    else:
        qdec = jnp.exp(lg * (pos + 1.0))
        kdec = jnp.exp(lg * ((c - 1.0) - pos))
    cdec = jnp.exp(lg * jnp.full((1, 1), float(c), F32))

    nchunk = q_ref.shape[1] // c
    order = list(reversed(range(nchunk))) if reverse else list(range(nchunk))

    def local_terms(ci):
        rows = slice(ci * c, (ci + 1) * c)
        kc = k_ref[0, rows, :]
        vc = v_ref[0, rows, :]
        if reverse:
            s = lax.dot_general(q_ref[0, rows, :], kc, (((1,), (1,)), ((), ())), preferred_element_type=F32)
            ibuf[rows, :] = jnp.dot((s * dmat).astype(BF16), vc, preferred_element_type=F32)
        kd = (kc.astype(F32) * kdec).astype(BF16)
        ubuf[ci] = lax.dot_general(kd, vc, (((0,), (0,)), ((), ())), preferred_element_type=F32)

    local_terms(order[0])
    st = state[...]
    for n, ci in enumerate(order):
        if n + 1 < nchunk:
            local_terms(order[n + 1])
        rows = slice(ci * c, (ci + 1) * c)
        cross = jnp.dot(q_ref[0, rows, :], st.astype(BF16), preferred_element_type=F32) * qdec
        st = st * cdec + ubuf[ci]
        if ci == order[-1]:
            state[...] = st
        if reverse:
            o_ref[0, rows, :] = (ibuf[rows, :] + cross).astype(BF16)
        else:
            o = part_ref[0, rows, :].astype(F32) + cross
            mu = jnp.mean(o, axis=-1, keepdims=True)
            dlt = o - mu
            var = jnp.mean(dlt * dlt, axis=-1, keepdims=True)
            o_ref[0, rows, :] = ((dlt * lax.rsqrt(var + EPS)) * gnw_ref[...]).astype(BF16)


def _retention(lgs, q, k, v, s0, extra, *, reverse, t):
    b, l, _ = q.shape
    nt = l // t
    tmap = (lambda i, h, j: (i, nt - 1 - j, h)) if reverse else (lambda i, h, j: (i, j, h))
    in_specs = [pl.BlockSpec(memory_space=pltpu.SMEM)] * len(lgs)
    in_specs += [pl.BlockSpec((1, t, RET_QK_DIM), tmap),
                 pl.BlockSpec((1, t, RET_QK_DIM), tmap),
                 pl.BlockSpec((1, t, RET_V_DIM), tmap),
                 pl.BlockSpec((1, 1, RET_QK_DIM, RET_V_DIM), lambda i, h, j: (i, h, 0, 0))]
    scratch = [pltpu.VMEM((RET_QK_DIM, RET_V_DIM), F32),
               pltpu.VMEM((t // RET_SUB, RET_QK_DIM, RET_V_DIM), F32)]
    if reverse:
        scratch.append(pltpu.VMEM((t, RET_V_DIM), F32))
    else:
        in_specs += [pl.BlockSpec((1, t, RET_V_DIM), tmap),
                     pl.BlockSpec((1, RET_V_DIM), lambda i, h, j: (0, h))]
    return pl.pallas_call(
        functools.partial(_ret_kernel, reverse=reverse),
        grid=(b, RET_HEADS, nt),
        in_specs=in_specs,
        out_specs=pl.BlockSpec((1, t, RET_V_DIM), tmap),
        out_shape=jax.ShapeDtypeStruct((b, l, RET_V_W), BF16),
        scratch_shapes=scratch,
        compiler_params=_params(("parallel", "parallel", "arbitrary")),
        name="ret_bwd" if reverse else "ret_fwd",
    )(*lgs, q, k, v, s0, *extra)


def _mixer_kernel(on_ref, rg_ref, ys_ref, ga_ref, gb_ref, h_ref, mods_ref, wro_ref, wco_ref, wo_ref, o_ref):
    y_conv = jnp.dot(ys_ref[0], wco_ref[...], preferred_element_type=F32)
    gated = (rg_ref[0].astype(F32) * on_ref[0].astype(F32)).astype(BF16)
    y_ret = jnp.dot(gated, wro_ref[...], preferred_element_type=F32)
    merged = (ga_ref[0].astype(F32) * y_ret + gb_ref[0].astype(F32) * y_conv).astype(BF16)
    m = jnp.dot(merged, wo_ref[...], preferred_element_type=F32)
    g2 = mods_ref[0, 5:6, :]
    o_ref[0] = h_ref[0] + g2 * m


def _mixer(on, srg, ys, sga, sgb, h, mods, wro, wco, wo, *, tm):
    b, l, d = h.shape
    tok = lambda i, t: (i, t, 0)
    return pl.pallas_call(
        _mixer_kernel,
        grid=(b, l // tm),
        in_specs=[pl.BlockSpec((1, tm, RET_V_W), tok),
                  pl.BlockSpec((1, tm, RET_V_W), tok),
                  pl.BlockSpec((1, tm, d), tok),
                  pl.BlockSpec((1, tm, d), tok),
                  pl.BlockSpec((1, tm, d), tok),
                  pl.BlockSpec((1, tm, d), tok),
                  pl.BlockSpec((1, N_MOD, d), lambda i, t: (i, 0, 0)),
                  _resident(wro.shape),
                  _resident(wco.shape),
                  _resident(wo.shape)],
        out_specs=pl.BlockSpec((1, tm, d), tok),
        out_shape=jax.ShapeDtypeStruct((b, l, d), F32),
        compiler_params=_params(("parallel", "parallel")),
        name="mixer",
    )(on, srg, ys, sga, sgb, h, mods, wro, wco, wo)


def kernel(x, c, ctx, c_ctx, w_mod, b_mod, norm_ffn1, w_ffn1_in, w_ffn1_out, norm_mix, w_in, ret_decay_f, ret_decay_b, ret_gn_w, w_ret_out, conv_w, conv_b, conv_ln_w, conv_ln_b, w_conv_out, w_out, norm_ffn2, w_ffn2_in, w_ffn2_out, final_norm):
    assert w_mod.shape[0] == 1, "single-layer block"
    b, l, d = x.shape
    assert l % max(FFN_TM, RET_T, MIX_TM) == 0 and d == D_MODEL

    cc = jnp.zeros((SUBLANES, d), F32).at[:b].set(c).at[b].set(c_ctx)
    mods_all = _mods(cc, w_mod[0], b_mod[0])
    mods = mods_all[:b].reshape(b, N_MOD, d)
    mods_c = mods_all[b:b + 1].reshape(1, N_MOD, d)

    col = jnp.arange(IN_WIDTH)
    kscale = jnp.where((col >= K0) & (col < V0), RET_QK_DIM ** -0.5, 1.0).astype(F32)
    w_in_bf = (w_in[0] * kscale[None, :]).astype(BF16)
    w1a, w1b = w_ffn1_in[0].astype(BF16), w_ffn1_out[0].astype(BF16)
    w2a, w2b = w_ffn2_in[0].astype(BF16), w_ffn2_out[0].astype(BF16)
    lgf = jax.nn.log_sigmoid(ret_decay_f[0].astype(F32))
    lgb = jax.nn.log_sigmoid(ret_decay_b[0].astype(F32))

    h, u = _ffn(x, mods, norm_ffn1[0], w1a, w1b, norm_mix[0], row0=0, final=False, tm=FFN_TM)
    _, uc = _ffn(ctx, mods_c, norm_ffn1[0], w1a, w1b, norm_mix[0], row0=0, final=False, tm=FFN_TM)

    rf, rb = _ctx_state(uc, w_in_bf, lgf, lgb)

    q, k, v, srg, ys, sga, sgb = _inproj(u, w_in_bf, _rope_tables(l), conv_w[0], conv_b[0],
                                         conv_ln_w[0], conv_ln_b[0], tm=INPROJ_TM)

    part = _retention((lgb, lgf), q, k, v, rb, (), reverse=True, t=RET_T)
    on = _retention((lgf,), q, k, v, rf, (part, ret_gn_w[0].reshape(1, RET_V_W)), reverse=False, t=RET_T)

    h2 = _mixer(on, srg, ys, sga, sgb, h, mods, w_ret_out[0].astype(BF16), w_conv_out[0].astype(BF16),
                w_out[0].astype(BF16), tm=MIX_TM)

    return _ffn(h2, mods, norm_ffn2[0], w2a, w2b, final_norm, row0=6, final=True, tm=FFN_TM)
```

```python
import functools

import jax
import jax.numpy as jnp
from jax import lax
from jax.experimental import pallas as pl
from jax.experimental.pallas import tpu as pltpu

F32 = jnp.float32
BF16 = jnp.bfloat16

D_MODEL = 1024
GRID_W = 64
RET_HEADS = 4
RET_QK_DIM = 256
RET_V_DIM = 512
CONV_WIDTH = 31
CONV_PAD = CONV_WIDTH // 2
D_FF = 2816
ROPE_BASE = 10000.0
EPS = 1e-6
N_MOD = 9

RET_QK_W = RET_HEADS * RET_QK_DIM
RET_V_W = RET_HEADS * RET_V_DIM
K0 = RET_QK_W
V0 = 2 * RET_QK_W
G0 = V0 + RET_V_W
C0 = G0 + RET_V_W
GA0 = C0 + 2 * D_MODEL
GB0 = GA0 + D_MODEL
IN_WIDTH = GB0 + D_MODEL

V7X_VMEM_LIMIT_BYTES = 56 * 1024 * 1024
SUBLANES = 8
LANES = 128
BF16_ROWS = 16
HALO = 2 * BF16_ROWS

MODS_TN = 2304
CAST_CHUNKS = 16
FFN_CHUNK = 256
FFN_TM = 512
INPROJ_TM = 512
INPROJ_TN = 256
MIX_TM = 512
RET_T = 2048
RET_SUB = 256
CONV_RB = 64


def _resident(shape):
    nd = len(shape)
    return pl.BlockSpec(shape, lambda *_: (0,) * nd, pipeline_mode=pl.Buffered(1))


def _params(sem):
    return pltpu.CompilerParams(dimension_semantics=sem, vmem_limit_bytes=V7X_VMEM_LIMIT_BYTES)


def _first_step():
    return (pl.program_id(0) == 0) & (pl.program_id(1) == 0)


def _load_cast_bf16(src_hbm, dst, stage, sems, scaled_cols=None):
    rows = stage.shape[1]
    n = src_hbm.shape[0] // rows
    assert n * rows == src_hbm.shape[0]

    def copy(i):
        return pltpu.make_async_copy(src_hbm.at[pl.ds(i * rows, rows)], stage.at[i % 2], sems.at[i % 2])

    copy(0).start()
    for i in range(n):
        if i + 1 < n:
            copy(i + 1).start()
        copy(i).wait()
        rs = slice(i * rows, (i + 1) * rows)
        if scaled_cols is None:
            dst[rs, :] = stage[i % 2].astype(BF16)
        else:
            c0, c1, s = scaled_cols
            dst[rs, :c0] = stage[i % 2, :, :c0].astype(BF16)
            dst[rs, c0:c1] = (stage[i % 2, :, c0:c1] * s).astype(BF16)
            dst[rs, c1:] = stage[i % 2, :, c1:].astype(BF16)


def _rms_mod(x, w, scale, shift):
    y = x * lax.rsqrt(jnp.mean(x * x, axis=-1, keepdims=True) + EPS)
    return (y * w) * (1.0 + scale) + shift


def _layernorm(x, w, b):
    mu = jnp.mean(x, axis=-1, keepdims=True)
    d = x - mu
    var = jnp.mean(d * d, axis=-1, keepdims=True)
    return (d * lax.rsqrt(var + EPS)) * w + b


def _mods_kernel(c_ref, w_ref, b_ref, o_ref):
    c = c_ref[...]
    s = c * jax.nn.sigmoid(c)
    o_ref[...] = jnp.dot(s, w_ref[...], preferred_element_type=F32,
                         precision=lax.Precision.HIGHEST) + b_ref[...]


def _mods(cc, w_mod, b_mod):
    rows, d = cc.shape
    n = w_mod.shape[1]
    tn = MODS_TN
    return pl.pallas_call(
        _mods_kernel,
        grid=(n // tn,),
        in_specs=[pl.BlockSpec((rows, d), lambda j: (0, 0)),
                  pl.BlockSpec((d, tn), lambda j: (0, j)),
                  pl.BlockSpec((1, tn), lambda j: (0, j))],
        out_specs=pl.BlockSpec((rows, tn), lambda j: (0, j)),
        out_shape=jax.ShapeDtypeStruct((rows, n), F32),
        compiler_params=_params(("arbitrary",)),
        name="mods",
    )(cc, w_mod, b_mod.reshape(1, n))


def _ffn_kernel(x_ref, mods_ref, nw_ref, w1_hbm, w2_hbm, n2w_ref, *rest, row0, final):
    n_out = 1 if final else 2
    out_refs = rest[:n_out]
    w1_ref, w2_ref, stage1, stage2, sems = rest[n_out:]

    @pl.when(_first_step())
    def _():
        _load_cast_bf16(w1_hbm, w1_ref, stage1, sems.at[0])
        _load_cast_bf16(w2_hbm, w2_ref, stage2, sems.at[1])

    x = x_ref[0]
    sh = mods_ref[0, row0:row0 + 1, :]
    sc = mods_ref[0, row0 + 1:row0 + 2, :]
    g = mods_ref[0, row0 + 2:row0 + 3, :]
    xm = _rms_mod(x, nw_ref[...], sc, sh).astype(BF16)
    ck = FFN_CHUNK
    acc = jnp.zeros_like(x)
    for c0 in range(0, D_FF, ck):
        a = jnp.dot(xm, w1_ref[:, c0:c0 + ck], preferred_element_type=F32)
        b = jnp.dot(xm, w1_ref[:, D_FF + c0:D_FF + c0 + ck], preferred_element_type=F32)
        act = (a * jax.nn.sigmoid(a) * b).astype(BF16)
        acc = acc + jnp.dot(act, w2_ref[c0:c0 + ck, :], preferred_element_type=F32)
    h = x + (0.5 * g) * acc
    if final:
        (o_ref,) = out_refs
        y = h * lax.rsqrt(jnp.mean(h * h, axis=-1, keepdims=True) + EPS)
        o_ref[0] = y * n2w_ref[...]
    else:
        h_ref, u_ref = out_refs
        h_ref[0] = h
        sh2 = mods_ref[0, row0 + 3:row0 + 4, :]
        sc2 = mods_ref[0, row0 + 4:row0 + 5, :]
        u_ref[0] = _rms_mod(h, n2w_ref[...], sc2, sh2).astype(BF16)


def _ffn(x, mods, norm_w, w1, w2, norm2_w, *, row0, final, tm):
    b, l, d = x.shape
    tm = min(tm, l)
    mods_map = (lambda i, t: (i, 0, 0)) if mods.shape[0] == b else (lambda i, t: (0, 0, 0))
    tok = lambda i, t: (i, t, 0)
    hbm = pl.BlockSpec(memory_space=pl.ANY)
    in_specs = [pl.BlockSpec((1, tm, d), tok),
                pl.BlockSpec((1, N_MOD, d), mods_map),
                _resident((1, d)),
                hbm,
                hbm,
                _resident((1, d))]
    scratch = [pltpu.VMEM(w1.shape, BF16),
               pltpu.VMEM(w2.shape, BF16),
               pltpu.VMEM((2, w1.shape[0] // CAST_CHUNKS, w1.shape[1]), F32),
               pltpu.VMEM((2, w2.shape[0] // CAST_CHUNKS, w2.shape[1]), F32),
               pltpu.SemaphoreType.DMA((2, 2))]
    if final:
        out_specs = pl.BlockSpec((1, tm, d), tok)
        out_shape = jax.ShapeDtypeStruct((b, l, d), F32)
    else:
        out_specs = [pl.BlockSpec((1, tm, d), tok), pl.BlockSpec((1, tm, d), tok)]
        out_shape = [jax.ShapeDtypeStruct((b, l, d), F32), jax.ShapeDtypeStruct((b, l, d), BF16)]
    return pl.pallas_call(
        functools.partial(_ffn_kernel, row0=row0, final=final),
        grid=(b, l // tm),
        in_specs=in_specs,
        out_specs=out_specs,
        out_shape=out_shape,
        scratch_shapes=scratch,
        compiler_params=_params(("arbitrary", "arbitrary")),
        name="ffn_final" if final else "ffn_mix",
    )(x, mods, norm_w.reshape(1, d), w1, w2, norm2_w.reshape(1, d))


def _ctx_state_kernel(lgf_ref, lgb_ref, u_ref, wk_ref, wv_ref, rf_ref, rb_ref):
    h = pl.program_id(1)
    u = u_ref[0]
    n = u.shape[0]
    k = jnp.dot(u, wk_ref[...].astype(BF16), preferred_element_type=F32) * (RET_QK_DIM ** -0.5)
    v = jnp.dot(u, wv_ref[...].astype(BF16), preferred_element_type=F32).astype(BF16)
    pos = lax.broadcasted_iota(jnp.int32, (n, 1), 0).astype(F32)
    wf = jnp.exp(lgf_ref[h] * ((n - 1.0) - pos))
    wb = jnp.exp(lgb_ref[h] * pos)
    tdot = lambda a, c: lax.dot_general(a, c, (((0,), (0,)), ((), ())), preferred_element_type=F32)
    rf_ref[0, 0] = tdot((k * wf).astype(BF16), v)
    rb_ref[0, 0] = tdot((k * wb).astype(BF16), v)


def _ctx_state(uc, w_in, lgf, lgb):
    b, n, d = uc.shape
    smem = pl.BlockSpec(memory_space=pltpu.SMEM)
    st = jax.ShapeDtypeStruct((b, RET_HEADS, RET_QK_DIM, RET_V_DIM), F32)
    st_spec = pl.BlockSpec((1, 1, RET_QK_DIM, RET_V_DIM), lambda i, h: (i, h, 0, 0))
    return pl.pallas_call(
        _ctx_state_kernel,
        grid=(b, RET_HEADS),
        in_specs=[smem, smem,
                  pl.BlockSpec((1, n, d), lambda i, h: (i, 0, 0)),
                  pl.BlockSpec((d, RET_QK_DIM), lambda i, h: (0, K0 // RET_QK_DIM + h)),
                  pl.BlockSpec((d, RET_V_DIM), lambda i, h: (0, V0 // RET_V_DIM + h))],
        out_specs=[st_spec, st_spec],
        out_shape=[st, st],
        compiler_params=_params(("parallel", "parallel")),
        name="ctx_state",
    )(lgf, lgb, uc, w_in, w_in)


def _depthwise_conv(ybuf, cw_ref, cbuf, c0, r0, zero):
    rows = CONV_RB + HALO
    base = BF16_ROWS - CONV_PAD
    cs = slice(c0, c0 + LANES)
    src = ybuf[r0:r0 + rows, cs]
    parts = [None] * (HALO // SUBLANES)
    for r in range(SUBLANES):
        zr = src if r == 0 else pltpu.roll(src, rows - r, 0)
        for ai, a0 in enumerate(range(0, HALO, SUBLANES)):
            j = a0 + r - base
            if 0 <= j < CONV_WIDTH:
                term = zr[a0:a0 + CONV_RB, :] * cw_ref[j:j + 1, cs]
                parts[ai] = term if parts[ai] is None else parts[ai] + term
    acc = (parts[0] + parts[1]) + (parts[2] + parts[3])
    cbuf[r0:r0 + CONV_RB, cs] = acc + jnp.tile(zero, (CONV_RB // SUBLANES, 1))


def _inproj_kernel(u_ref, uprev_ref, unext_ref, w_hbm, rcos_ref, rsin_ref, ccos_ref, csin_ref,
                   cw_ref, cb_ref, lnw_ref, lnb_ref,
                   q_ref, k_ref, v_ref, rg_ref, ys_ref, ga_ref, gb_ref,
                   uext, ybuf, cbuf, w_ref, stage, sems):
    @pl.when(_first_step())
    def _():
        _load_cast_bf16(w_hbm, w_ref, stage, sems, scaled_cols=(K0, V0, RET_QK_DIM ** -0.5))

    t = pl.program_id(1)
    tm = u_ref.shape[1]
    hb = BF16_ROWS
    u = u_ref[0]

    def proj(c0):
        return jnp.dot(u, w_ref[:, c0:c0 + INPROJ_TN], preferred_element_type=F32)

    def rope(val, ref, o0):
        for gi in range(tm // GRID_W):
            rs = slice(gi * GRID_W, (gi + 1) * GRID_W)
            for l0 in range(0, INPROJ_TN, LANES):
                if (o0 + l0) % RET_QK_DIM == 0:
                    cs, sn = rcos_ref[0, gi:gi + 1, :], rsin_ref[0, gi:gi + 1, :]
                else:
                    cs, sn = ccos_ref[...], csin_ref[...]
                slab = val[rs, l0:l0 + LANES]
                ref[0, rs, o0 + l0:o0 + l0 + LANES] = (
                    slab * cs + pltpu.roll(slab, LANES // 2, 1) * sn).astype(BF16)

    def piece(c0):
        val = proj(c0)
        if c0 < K0:
            rope(val, q_ref, c0)
        elif c0 < V0:
            rope(val, k_ref, c0 - K0)
        elif c0 < G0:
            v_ref[0, :, c0 - V0:c0 - V0 + INPROJ_TN] = val.astype(BF16)
        elif c0 < C0:
            rg_ref[0, :, c0 - G0:c0 - G0 + INPROJ_TN] = (val * jax.nn.sigmoid(val)).astype(BF16)
        elif c0 < GB0:
            ga_ref[0, :, c0 - GA0:c0 - GA0 + INPROJ_TN] = jax.nn.sigmoid(val).astype(BF16)
        else:
            gb_ref[0, :, c0 - GB0:c0 - GB0 + INPROJ_TN] = jax.nn.sigmoid(val).astype(BF16)
        return val

    uext[0:hb, :] = jnp.where(t > 0, uprev_ref[0], jnp.zeros_like(uprev_ref[0]))
    uext[hb:hb + tm, :] = u
    uext[hb + tm:, :] = jnp.where(t < pl.num_programs(1) - 1, unext_ref[0], jnp.zeros_like(unext_ref[0]))
    ue = uext[...]
    ca = jnp.dot(ue, w_ref[:, C0:C0 + D_MODEL], preferred_element_type=F32)
    cg = jnp.dot(ue, w_ref[:, C0 + D_MODEL:GA0], preferred_element_type=F32)
    ybuf[...] = ca * jax.nn.sigmoid(cg)

    cols = [c0 for c0 in range(0, IN_WIDTH, INPROJ_TN) if not C0 <= c0 < GA0]
    blocks = [(c0, r0) for c0 in range(0, D_MODEL, LANES) for r0 in range(0, tm, CONV_RB)]
    per_tile = len(blocks) // len(cols)
    assert per_tile * len(cols) == len(blocks)
    zero = jnp.zeros((SUBLANES, LANES), F32)
    for i, c0 in enumerate(cols):
        for b0, r0 in blocks[i * per_tile:(i + 1) * per_tile]:
            _depthwise_conv(ybuf, cw_ref, cbuf, b0, r0, zero)
        bits = piece(c0)[0:SUBLANES, 0:LANES].astype(jnp.int32)
        zero = lax.shift_right_logical(lax.shift_right_logical(bits, 16), 16).astype(F32)

    yn = _layernorm(cbuf[...] + cb_ref[...], lnw_ref[...], lnb_ref[...])
    ys_ref[0] = (yn * jax.nn.sigmoid(yn)).astype(BF16)


def _inproj(u, w_in, rope, conv_w, conv_b, ln_w, ln_b, *, tm):
    b, l, d = u.shape
    hb = BF16_ROWS
    gpt = tm // GRID_W
    rcos, rsin, ccos, csin = rope
    tok = lambda i, t: (i, t, 0)
    prev = lambda i, t: (i, jnp.maximum(t * (tm // hb) - 1, 0), 0)
    nxt = lambda i, t: (i, jnp.minimum((t + 1) * (tm // hb), l // hb - 1), 0)
    rtab = lambda i, t: (t, 0, 0)
    vec = lambda a: a.reshape(1, d)
    widths = (RET_QK_W, RET_QK_W, RET_V_W, RET_V_W, D_MODEL, D_MODEL, D_MODEL)
    return pl.pallas_call(
        _inproj_kernel,
        grid=(b, l // tm),
        in_specs=[pl.BlockSpec((1, tm, d), tok),
                  pl.BlockSpec((1, hb, d), prev),
                  pl.BlockSpec((1, hb, d), nxt),
                  pl.BlockSpec(memory_space=pl.ANY),
                  pl.BlockSpec((1, gpt, LANES), rtab),
                  pl.BlockSpec((1, gpt, LANES), rtab),
                  _resident(ccos.shape),
                  _resident(csin.shape),
                  _resident(conv_w.shape),
                  _resident((1, d)), _resident((1, d)), _resident((1, d))],
        out_specs=[pl.BlockSpec((1, tm, w), tok) for w in widths],
        out_shape=[jax.ShapeDtypeStruct((b, l, w), BF16) for w in widths],
        scratch_shapes=[pltpu.VMEM((tm + HALO, d), BF16),
                        pltpu.VMEM((tm + HALO, d), F32),
                        pltpu.VMEM((tm, d), F32),
                        pltpu.VMEM(w_in.shape, BF16),
                        pltpu.VMEM((2, d // (2 * CAST_CHUNKS), IN_WIDTH), F32),
                        pltpu.SemaphoreType.DMA((2,))],
        compiler_params=_params(("arbitrary", "arbitrary")),
        name="inproj",
    )(u, u, u, w_in, rcos.reshape(l // tm, gpt, LANES), rsin.reshape(l // tm, gpt, LANES),
      ccos, csin, conv_w, vec(conv_b), vec(ln_w), vec(ln_b))


def _rope_tables(l):
    half = RET_QK_DIM // 2
    inv = ROPE_BASE ** (-jnp.arange(0, half, 2, dtype=F32) / half)

    def tabs(pos):
        ang = pos[:, None] * inv[None, :]
        return (jnp.concatenate([jnp.cos(ang), jnp.cos(ang)], axis=-1),
                jnp.concatenate([-jnp.sin(ang), jnp.sin(ang)], axis=-1))

    rcos, rsin = tabs(jnp.arange(l // GRID_W, dtype=F32))
    ccos, csin = tabs(jnp.arange(GRID_W, dtype=F32))
    return rcos, rsin, ccos, csin


def _ret_kernel(*refs, reverse):
    if reverse:
        lg_ref, lgf_ref, q_ref, k_ref, v_ref, s0_ref, o_ref, state, ubuf, ibuf = refs
    else:
        lg_ref, q_ref, k_ref, v_ref, s0_ref, part_ref, gnw_ref, o_ref, state, ubuf = refs
    h = pl.program_id(1)

    @pl.when(pl.program_id(2) == 0)
    def _():
        state[...] = s0_ref[0, 0]

    c = RET_SUB
    lg = lg_ref[h]
    pos = lax.broadcasted_iota(jnp.int32, (c, 1), 0).astype(F32)
    if reverse:
        diff = (lax.broadcasted_iota(jnp.int32, (c, c), 0)
                - lax.broadcasted_iota(jnp.int32, (c, c), 1)).astype(F32)
        dmat = jnp.where(diff < 0, jnp.exp(lg * jnp.maximum(-diff, 0.0)),
                         jnp.exp(lgf_ref[h] * jnp.maximum(diff, 0.0)))
        qdec = jnp.exp(lg * (c - pos))
        kdec = jnp.exp(lg * pos)
    else:
        qdec = jnp.exp(lg * (pos + 1.0))
        kdec = jnp.exp(lg * ((c - 1.0) - pos))
    cdec = jnp.exp(lg * jnp.full((1, 1), float(c), F32))

    nchunk = q_ref.shape[1] // c
    order = list(reversed(range(nchunk))) if reverse else list(range(nchunk))

    def local_terms(ci):
        rows = slice(ci * c, (ci + 1) * c)
        kc = k_ref[0, rows, :]
        vc = v_ref[0, rows, :]
        if reverse:
            s = lax.dot_general(q_ref[0, rows, :], kc, (((1,), (1,)), ((), ())), preferred_element_type=F32)
            ibuf[rows, :] = jnp.dot((s * dmat).astype(BF16), vc, preferred_element_type=F32)
        kd = (kc.astype(F32) * kdec).astype(BF16)
        ubuf[ci] = lax.dot_general(kd, vc, (((0,), (0,)), ((), ())), preferred_element_type=F32)

    local_terms(order[0])
    st = state[...]
    for n, ci in enumerate(order):
        if n + 1 < nchunk:
            local_terms(order[n + 1])
        rows = slice(ci * c, (ci + 1) * c)
        cross = jnp.dot(q_ref[0, rows, :], st.astype(BF16), preferred_element_type=F32) * qdec
        st = st * cdec + ubuf[ci]
        if ci == order[-1]:
            state[...] = st
        if reverse:
            o_ref[0, rows, :] = (ibuf[rows, :] + cross).astype(BF16)
        else:
            o = part_ref[0, rows, :].astype(F32) + cross
            mu = jnp.mean(o, axis=-1, keepdims=True)
            dlt = o - mu
            var = jnp.mean(dlt * dlt, axis=-1, keepdims=True)
            o_ref[0, rows, :] = ((dlt * lax.rsqrt(var + EPS)) * gnw_ref[...]).astype(BF16)


def _retention(lgs, q, k, v, s0, extra, *, reverse, t):
    b, l, _ = q.shape
    nt = l // t
    tmap = (lambda i, h, j: (i, nt - 1 - j, h)) if reverse else (lambda i, h, j: (i, j, h))
    in_specs = [pl.BlockSpec(memory_space=pltpu.SMEM)] * len(lgs)
    in_specs += [pl.BlockSpec((1, t, RET_QK_DIM), tmap),
                 pl.BlockSpec((1, t, RET_QK_DIM), tmap),
                 pl.BlockSpec((1, t, RET_V_DIM), tmap),
                 pl.BlockSpec((1, 1, RET_QK_DIM, RET_V_DIM), lambda i, h, j: (i, h, 0, 0))]
    scratch = [pltpu.VMEM((RET_QK_DIM, RET_V_DIM), F32),
               pltpu.VMEM((t // RET_SUB, RET_QK_DIM, RET_V_DIM), F32)]
    if reverse:
        scratch.append(pltpu.VMEM((t, RET_V_DIM), F32))
    else:
        in_specs += [pl.BlockSpec((1, t, RET_V_DIM), tmap),
                     pl.BlockSpec((1, RET_V_DIM), lambda i, h, j: (0, h))]
    return pl.pallas_call(
        functools.partial(_ret_kernel, reverse=reverse),
        grid=(b, RET_HEADS, nt),
        in_specs=in_specs,
        out_specs=pl.BlockSpec((1, t, RET_V_DIM), tmap),
        out_shape=jax.ShapeDtypeStruct((b, l, RET_V_W), BF16),
        scratch_shapes=scratch,
        compiler_params=_params(("parallel", "parallel", "arbitrary")),
        name="ret_bwd" if reverse else "ret_fwd",
    )(*lgs, q, k, v, s0, *extra)


def _mixer_kernel(on_ref, rg_ref, ys_ref, ga_ref, gb_ref, h_ref, mods_ref, wro_hbm, wco_hbm, wo_hbm, o_ref,
                  wro_ref, wco_ref, wo_ref, stage, sems):
    @pl.when(_first_step())
    def _():
        _load_cast_bf16(wro_hbm, wro_ref, stage, sems)
        _load_cast_bf16(wco_hbm, wco_ref, stage, sems)
        _load_cast_bf16(wo_hbm, wo_ref, stage, sems)

    y_conv = jnp.dot(ys_ref[0], wco_ref[...], preferred_element_type=F32)
    gated = (rg_ref[0].astype(F32) * on_ref[0].astype(F32)).astype(BF16)
    y_ret = jnp.dot(gated, wro_ref[...], preferred_element_type=F32)
    merged = (ga_ref[0].astype(F32) * y_ret + gb_ref[0].astype(F32) * y_conv).astype(BF16)
    m = jnp.dot(merged, wo_ref[...], preferred_element_type=F32)
    g2 = mods_ref[0, 5:6, :]
    o_ref[0] = h_ref[0] + g2 * m


def _mixer(on, srg, ys, sga, sgb, h, mods, wro, wco, wo, *, tm):
    b, l, d = h.shape
    tok = lambda i, t: (i, t, 0)
    hbm = pl.BlockSpec(memory_space=pl.ANY)
    return pl.pallas_call(
        _mixer_kernel,
        grid=(b, l // tm),
        in_specs=[pl.BlockSpec((1, tm, RET_V_W), tok),
                  pl.BlockSpec((1, tm, RET_V_W), tok),
                  pl.BlockSpec((1, tm, d), tok),
                  pl.BlockSpec((1, tm, d), tok),
                  pl.BlockSpec((1, tm, d), tok),
                  pl.BlockSpec((1, tm, d), tok),
                  pl.BlockSpec((1, N_MOD, d), lambda i, t: (i, 0, 0)),
                  hbm, hbm, hbm],
        out_specs=pl.BlockSpec((1, tm, d), tok),
        out_shape=jax.ShapeDtypeStruct((b, l, d), F32),
        scratch_shapes=[pltpu.VMEM(wro.shape, BF16),
                        pltpu.VMEM(wco.shape, BF16),
                        pltpu.VMEM(wo.shape, BF16),
                        pltpu.VMEM((2, d // (CAST_CHUNKS // 2), d), F32),
                        pltpu.SemaphoreType.DMA((2,))],
        compiler_params=_params(("arbitrary", "arbitrary")),
        name="mixer",
    )(on, srg, ys, sga, sgb, h, mods, wro, wco, wo)


def kernel(x, c, ctx, c_ctx, w_mod, b_mod, norm_ffn1, w_ffn1_in, w_ffn1_out, norm_mix, w_in, ret_decay_f, ret_decay_b, ret_gn_w, w_ret_out, conv_w, conv_b, conv_ln_w, conv_ln_b, w_conv_out, w_out, norm_ffn2, w_ffn2_in, w_ffn2_out, final_norm):
    assert w_mod.shape[0] == 1, "single-layer block"
    b, l, d = x.shape
    assert l % max(FFN_TM, RET_T, MIX_TM) == 0 and d == D_MODEL

    cc = jnp.zeros((SUBLANES, d), F32).at[:b].set(c).at[b].set(c_ctx)
    mods_all = _mods(cc, w_mod[0], b_mod[0])
    mods = mods_all[:b].reshape(b, N_MOD, d)
    mods_c = mods_all[b:b + 1].reshape(1, N_MOD, d)

    lgf = jax.nn.log_sigmoid(ret_decay_f[0].astype(F32))
    lgb = jax.nn.log_sigmoid(ret_decay_b[0].astype(F32))

    h, u = _ffn(x, mods, norm_ffn1[0], w_ffn1_in[0], w_ffn1_out[0], norm_mix[0], row0=0, final=False, tm=FFN_TM)
    _, uc = _ffn(ctx, mods_c, norm_ffn1[0], w_ffn1_in[0], w_ffn1_out[0], norm_mix[0], row0=0, final=False,
                 tm=FFN_TM)

    rf, rb = _ctx_state(uc, w_in[0], lgf, lgb)

    q, k, v, srg, ys, sga, sgb = _inproj(u, w_in[0], _rope_tables(l), conv_w[0], conv_b[0],
                                         conv_ln_w[0], conv_ln_b[0], tm=INPROJ_TM)

    part = _retention((lgb, lgf), q, k, v, rb, (), reverse=True, t=RET_T)
    on = _retention((lgf,), q, k, v, rf, (part, ret_gn_w[0].reshape(1, RET_V_W)), reverse=False, t=RET_T)

    h2 = _mixer(on, srg, ys, sga, sgb, h, mods, w_ret_out[0], w_conv_out[0], w_out[0], tm=MIX_TM)

    return _ffn(h2, mods, norm_ffn2[0], w_ffn2_in[0], w_ffn2_out[0], final_norm, row0=6, final=True, tm=FFN_TM)
```

```python
import functools

import jax
import jax.numpy as jnp
from jax import lax
from jax.experimental import pallas as pl
from jax.experimental.pallas import tpu as pltpu

F32 = jnp.float32
BF16 = jnp.bfloat16

D_MODEL = 1024
GRID_W = 64
RET_HEADS = 4
RET_QK_DIM = 256
RET_V_DIM = 512
CONV_WIDTH = 31
CONV_PAD = CONV_WIDTH // 2
D_FF = 2816
ROPE_BASE = 10000.0
EPS = 1e-6
N_MOD = 9

RET_QK_W = RET_HEADS * RET_QK_DIM
RET_V_W = RET_HEADS * RET_V_DIM
K0 = RET_QK_W
V0 = 2 * RET_QK_W
G0 = V0 + RET_V_W
C0 = G0 + RET_V_W
GA0 = C0 + 2 * D_MODEL
GB0 = GA0 + D_MODEL
IN_WIDTH = GB0 + D_MODEL

V7X_VMEM_LIMIT_BYTES = 56 * 1024 * 1024
SUBLANES = 8
LANES = 128
BF16_ROWS = 16
HALO = 2 * BF16_ROWS

MODS_TN = 2304
CAST_CHUNKS = 16
CAST_SLOTS = 4
FFN_CHUNK = 256
FFN_TM = 512
INPROJ_TM = 512
INPROJ_TN = 256
MIX_TM = 512
RET_T = 2048
RET_SUB = 256
CONV_RB = 64


def _resident(shape):
    nd = len(shape)
    return pl.BlockSpec(shape, lambda *_: (0,) * nd, pipeline_mode=pl.Buffered(1))


def _params(sem):
    return pltpu.CompilerParams(dimension_semantics=sem, vmem_limit_bytes=V7X_VMEM_LIMIT_BYTES)


def _first_step():
    return (pl.program_id(0) == 0) & (pl.program_id(1) == 0)


def _load_cast_bf16(src_hbm, dst, stage, sems, scaled_cols=None):
    nslot, rows = stage.shape[0], stage.shape[1]
    n = src_hbm.shape[0] // rows
    assert n * rows == src_hbm.shape[0] and n >= nslot

    def copy(i):
        return pltpu.make_async_copy(src_hbm.at[pl.ds(i * rows, rows)], stage.at[i % nslot], sems.at[i % nslot])

    for i in range(nslot - 1):
        copy(i).start()
    for i in range(n):
        if i + nslot - 1 < n:
            copy(i + nslot - 1).start()
        copy(i).wait()
        rs = slice(i * rows, (i + 1) * rows)
        if scaled_cols is None:
            dst[rs, :] = stage[i % nslot].astype(BF16)
        else:
            c0, c1, s = scaled_cols
            dst[rs, :c0] = stage[i % nslot, :, :c0].astype(BF16)
            dst[rs, c0:c1] = (stage[i % nslot, :, c0:c1] * s).astype(BF16)
            dst[rs, c1:] = stage[i % nslot, :, c1:].astype(BF16)


def _rms_mod(x, w, scale, shift):
    y = x * lax.rsqrt(jnp.mean(x * x, axis=-1, keepdims=True) + EPS)
    return (y * w) * (1.0 + scale) + shift


def _layernorm(x, w, b):
    mu = jnp.mean(x, axis=-1, keepdims=True)
    d = x - mu
    var = jnp.mean(d * d, axis=-1, keepdims=True)
    return (d * lax.rsqrt(var + EPS)) * w + b


def _mods_kernel(c_ref, w_ref, b_ref, o_ref):
    c = c_ref[...]
    s = c * jax.nn.sigmoid(c)
    o_ref[...] = jnp.dot(s, w_ref[...], preferred_element_type=F32,
                         precision=lax.Precision.HIGHEST) + b_ref[...]


def _mods(cc, w_mod, b_mod):
    rows, d = cc.shape
    n = w_mod.shape[1]
    tn = MODS_TN
    return pl.pallas_call(
        _mods_kernel,
        grid=(n // tn,),
        in_specs=[pl.BlockSpec((rows, d), lambda j: (0, 0)),
                  pl.BlockSpec((d, tn), lambda j: (0, j)),
                  pl.BlockSpec((1, tn), lambda j: (0, j))],
        out_specs=pl.BlockSpec((rows, tn), lambda j: (0, j)),
        out_shape=jax.ShapeDtypeStruct((rows, n), F32),
        compiler_params=_params(("arbitrary",)),
        name="mods",
    )(cc, w_mod, b_mod.reshape(1, n))


def _ffn_kernel(x_ref, mods_ref, nw_ref, w1_hbm, w2_hbm, n2w_ref, *rest, row0, final):
    n_out = 1 if final else 2
    out_refs = rest[:n_out]
    w1_ref, w2_ref, stage1, stage2, sems = rest[n_out:]

    @pl.when(_first_step())
    def _():
        _load_cast_bf16(w1_hbm, w1_ref, stage1, sems.at[0])
        _load_cast_bf16(w2_hbm, w2_ref, stage2, sems.at[1])

    x = x_ref[0]
    sh = mods_ref[0, row0:row0 + 1, :]
    sc = mods_ref[0, row0 + 1:row0 + 2, :]
    g = mods_ref[0, row0 + 2:row0 + 3, :]
    xm = _rms_mod(x, nw_ref[...], sc, sh).astype(BF16)
    ck = FFN_CHUNK
    acc = jnp.zeros_like(x)
    for c0 in range(0, D_FF, ck):
        a = jnp.dot(xm, w1_ref[:, c0:c0 + ck], preferred_element_type=F32)
        b = jnp.dot(xm, w1_ref[:, D_FF + c0:D_FF + c0 + ck], preferred_element_type=F32)
        act = (a * jax.nn.sigmoid(a) * b).astype(BF16)
        acc = acc + jnp.dot(act, w2_ref[c0:c0 + ck, :], preferred_element_type=F32)
    h = x + (0.5 * g) * acc
    if final:
        (o_ref,) = out_refs
        y = h * lax.rsqrt(jnp.mean(h * h, axis=-1, keepdims=True) + EPS)
        o_ref[0] = y * n2w_ref[...]
    else:
        h_ref, u_ref = out_refs
        h_ref[0] = h
        sh2 = mods_ref[0, row0 + 3:row0 + 4, :]
        sc2 = mods_ref[0, row0 + 4:row0 + 5, :]
        u_ref[0] = _rms_mod(h, n2w_ref[...], sc2, sh2).astype(BF16)


def _ffn(x, mods, norm_w, w1, w2, norm2_w, *, row0, final, tm):
    b, l, d = x.shape
    tm = min(tm, l)
    mods_map = (lambda i, t: (i, 0, 0)) if mods.shape[0] == b else (lambda i, t: (0, 0, 0))
    tok = lambda i, t: (i, t, 0)
    hbm = pl.BlockSpec(memory_space=pl.ANY)
    in_specs = [pl.BlockSpec((1, tm, d), tok),
                pl.BlockSpec((1, N_MOD, d), mods_map),
                _resident((1, d)),
                hbm,
                hbm,
                _resident((1, d))]
    scratch = [pltpu.VMEM(w1.shape, BF16),
               pltpu.VMEM(w2.shape, BF16),
               pltpu.VMEM((CAST_SLOTS, w1.shape[0] // CAST_CHUNKS, w1.shape[1]), F32),
               pltpu.VMEM((CAST_SLOTS, w2.shape[0] // CAST_CHUNKS, w2.shape[1]), F32),
               pltpu.SemaphoreType.DMA((2, CAST_SLOTS))]
    if final:
        out_specs = pl.BlockSpec((1, tm, d), tok)
        out_shape = jax.ShapeDtypeStruct((b, l, d), F32)
    else:
        out_specs = [pl.BlockSpec((1, tm, d), tok), pl.BlockSpec((1, tm, d), tok)]
        out_shape = [jax.ShapeDtypeStruct((b, l, d), F32), jax.ShapeDtypeStruct((b, l, d), BF16)]
    return pl.pallas_call(
        functools.partial(_ffn_kernel, row0=row0, final=final),
        grid=(b, l // tm),
        in_specs=in_specs,
        out_specs=out_specs,
        out_shape=out_shape,
        scratch_shapes=scratch,
        compiler_params=_params(("arbitrary", "arbitrary")),
        name="ffn_final" if final else "ffn_mix",
    )(x, mods, norm_w.reshape(1, d), w1, w2, norm2_w.reshape(1, d))


def _ctx_state_kernel(lgf_ref, lgb_ref, u_ref, wk_ref, wv_ref, rf_ref, rb_ref):
    h = pl.program_id(1)
    u = u_ref[0]
    n = u.shape[0]
    k = jnp.dot(u, wk_ref[...].astype(BF16), preferred_element_type=F32) * (RET_QK_DIM ** -0.5)
    v = jnp.dot(u, wv_ref[...].astype(BF16), preferred_element_type=F32).astype(BF16)
    pos = lax.broadcasted_iota(jnp.int32, (n, 1), 0).astype(F32)
    wf = jnp.exp(lgf_ref[h] * ((n - 1.0) - pos))
    wb = jnp.exp(lgb_ref[h] * pos)
    tdot = lambda a, c: lax.dot_general(a, c, (((0,), (0,)), ((), ())), preferred_element_type=F32)
    rf_ref[0, 0] = tdot((k * wf).astype(BF16), v)
    rb_ref[0, 0] = tdot((k * wb).astype(BF16), v)


def _ctx_state(uc, w_in, lgf, lgb):
    b, n, d = uc.shape
    smem = pl.BlockSpec(memory_space=pltpu.SMEM)
    st = jax.ShapeDtypeStruct((b, RET_HEADS, RET_QK_DIM, RET_V_DIM), F32)
    st_spec = pl.BlockSpec((1, 1, RET_QK_DIM, RET_V_DIM), lambda i, h: (i, h, 0, 0))
    return pl.pallas_call(
        _ctx_state_kernel,
        grid=(b, RET_HEADS),
        in_specs=[smem, smem,
                  pl.BlockSpec((1, n, d), lambda i, h: (i, 0, 0)),
                  pl.BlockSpec((d, RET_QK_DIM), lambda i, h: (0, K0 // RET_QK_DIM + h)),
                  pl.BlockSpec((d, RET_V_DIM), lambda i, h: (0, V0 // RET_V_DIM + h))],
        out_specs=[st_spec, st_spec],
        out_shape=[st, st],
        compiler_params=_params(("parallel", "parallel")),
        name="ctx_state",
    )(lgf, lgb, uc, w_in, w_in)


def _depthwise_conv(ybuf, cw_ref, cbuf, c0, r0, zero):
    rows = CONV_RB + HALO
    base = BF16_ROWS - CONV_PAD
    cs = slice(c0, c0 + LANES)
    src = ybuf[r0:r0 + rows, cs]
    parts = [None] * (HALO // SUBLANES)
    for r in range(SUBLANES):
        zr = src if r == 0 else pltpu.roll(src, rows - r, 0)
        for ai, a0 in enumerate(range(0, HALO, SUBLANES)):
            j = a0 + r - base
            if 0 <= j < CONV_WIDTH:
                term = zr[a0:a0 + CONV_RB, :] * cw_ref[j:j + 1, cs]
                parts[ai] = term if parts[ai] is None else parts[ai] + term
    acc = (parts[0] + parts[1]) + (parts[2] + parts[3])
    cbuf[r0:r0 + CONV_RB, cs] = acc + jnp.tile(zero, (CONV_RB // SUBLANES, 1))


def _inproj_kernel(u_ref, uprev_ref, unext_ref, w_hbm, rcos_ref, rsin_ref, ccos_ref, csin_ref,
                   cw_ref, cb_ref, lnw_ref, lnb_ref,
                   q_ref, k_ref, v_ref, rg_ref, ys_ref, ga_ref, gb_ref,
                   uext, ybuf, cbuf, w_ref, stage, sems):
    @pl.when(_first_step())
    def _():
        _load_cast_bf16(w_hbm, w_ref, stage, sems, scaled_cols=(K0, V0, RET_QK_DIM ** -0.5))

    t = pl.program_id(1)
    tm = u_ref.shape[1]
    hb = BF16_ROWS
    u = u_ref[0]

    def proj(c0):
        return jnp.dot(u, w_ref[:, c0:c0 + INPROJ_TN], preferred_element_type=F32)

    def rope(val, ref, o0):
        for gi in range(tm // GRID_W):
            rs = slice(gi * GRID_W, (gi + 1) * GRID_W)
            for l0 in range(0, INPROJ_TN, LANES):
                if (o0 + l0) % RET_QK_DIM == 0:
                    cs, sn = rcos_ref[0, gi:gi + 1, :], rsin_ref[0, gi:gi + 1, :]
                else:
                    cs, sn = ccos_ref[...], csin_ref[...]
                slab = val[rs, l0:l0 + LANES]
                ref[0, rs, o0 + l0:o0 + l0 + LANES] = (
                    slab * cs + pltpu.roll(slab, LANES // 2, 1) * sn).astype(BF16)

    def piece(c0):
        val = proj(c0)
        if c0 < K0:
            rope(val, q_ref, c0)
        elif c0 < V0:
            rope(val, k_ref, c0 - K0)
        elif c0 < G0:
            v_ref[0, :, c0 - V0:c0 - V0 + INPROJ_TN] = val.astype(BF16)
        elif c0 < C0:
            rg_ref[0, :, c0 - G0:c0 - G0 + INPROJ_TN] = (val * jax.nn.sigmoid(val)).astype(BF16)
        elif c0 < GB0:
            ga_ref[0, :, c0 - GA0:c0 - GA0 + INPROJ_TN] = jax.nn.sigmoid(val).astype(BF16)
        else:
            gb_ref[0, :, c0 - GB0:c0 - GB0 + INPROJ_TN] = jax.nn.sigmoid(val).astype(BF16)
        return val

    uext[0:hb, :] = jnp.where(t > 0, uprev_ref[0], jnp.zeros_like(uprev_ref[0]))
    uext[hb:hb + tm, :] = u
    uext[hb + tm:, :] = jnp.where(t < pl.num_programs(1) - 1, unext_ref[0], jnp.zeros_like(unext_ref[0]))
    ue = uext[...]
    ca = jnp.dot(ue, w_ref[:, C0:C0 + D_MODEL], preferred_element_type=F32)
    cg = jnp.dot(ue, w_ref[:, C0 + D_MODEL:GA0], preferred_element_type=F32)
    ybuf[...] = ca * jax.nn.sigmoid(cg)

    cols = [c0 for c0 in range(0, IN_WIDTH, INPROJ_TN) if not C0 <= c0 < GA0]
    blocks = [(c0, r0) for c0 in range(0, D_MODEL, LANES) for r0 in range(0, tm, CONV_RB)]
    per_tile = len(blocks) // len(cols)
    assert per_tile * len(cols) == len(blocks)
    zero = jnp.zeros((SUBLANES, LANES), F32)
    for i, c0 in enumerate(cols):
        for b0, r0 in blocks[i * per_tile:(i + 1) * per_tile]:
            _depthwise_conv(ybuf, cw_ref, cbuf, b0, r0, zero)
        bits = piece(c0)[0:SUBLANES, 0:LANES].astype(jnp.int32)
        zero = lax.shift_right_logical(lax.shift_right_logical(bits, 16), 16).astype(F32)

    yn = _layernorm(cbuf[...] + cb_ref[...], lnw_ref[...], lnb_ref[...])
    ys_ref[0] = (yn * jax.nn.sigmoid(yn)).astype(BF16)


def _inproj(u, w_in, rope, conv_w, conv_b, ln_w, ln_b, *, tm):
    b, l, d = u.shape
    hb = BF16_ROWS
    gpt = tm // GRID_W
    rcos, rsin, ccos, csin = rope
    tok = lambda i, t: (i, t, 0)
    prev = lambda i, t: (i, jnp.maximum(t * (tm // hb) - 1, 0), 0)
    nxt = lambda i, t: (i, jnp.minimum((t + 1) * (tm // hb), l // hb - 1), 0)
    rtab = lambda i, t: (t, 0, 0)
    vec = lambda a: a.reshape(1, d)
    widths = (RET_QK_W, RET_QK_W, RET_V_W, RET_V_W, D_MODEL, D_MODEL, D_MODEL)
    return pl.pallas_call(
        _inproj_kernel,
        grid=(b, l // tm),
        in_specs=[pl.BlockSpec((1, tm, d), tok),
                  pl.BlockSpec((1, hb, d), prev),
                  pl.BlockSpec((1, hb, d), nxt),
                  pl.BlockSpec(memory_space=pl.ANY),
                  pl.BlockSpec((1, gpt, LANES), rtab),
                  pl.BlockSpec((1, gpt, LANES), rtab),
                  _resident(ccos.shape),
                  _resident(csin.shape),
                  _resident(conv_w.shape),
                  _resident((1, d)), _resident((1, d)), _resident((1, d))],
        out_specs=[pl.BlockSpec((1, tm, w), tok) for w in widths],
        out_shape=[jax.ShapeDtypeStruct((b, l, w), BF16) for w in widths],
        scratch_shapes=[pltpu.VMEM((tm + HALO, d), BF16),
                        pltpu.VMEM((tm + HALO, d), F32),
                        pltpu.VMEM((tm, d), F32),
                        pltpu.VMEM(w_in.shape, BF16),
                        pltpu.VMEM((CAST_SLOTS, d // (2 * CAST_CHUNKS), IN_WIDTH), F32),
                        pltpu.SemaphoreType.DMA((CAST_SLOTS,))],
        compiler_params=_params(("arbitrary", "arbitrary")),
        name="inproj",
    )(u, u, u, w_in, rcos.reshape(l // tm, gpt, LANES), rsin.reshape(l // tm, gpt, LANES),
      ccos, csin, conv_w, vec(conv_b), vec(ln_w), vec(ln_b))


def _rope_tables(l):
    half = RET_QK_DIM // 2
    inv = ROPE_BASE ** (-jnp.arange(0, half, 2, dtype=F32) / half)

    def tabs(pos):
        ang = pos[:, None] * inv[None, :]
        return (jnp.concatenate([jnp.cos(ang), jnp.cos(ang)], axis=-1),
                jnp.concatenate([-jnp.sin(ang), jnp.sin(ang)], axis=-1))

    rcos, rsin = tabs(jnp.arange(l // GRID_W, dtype=F32))
    ccos, csin = tabs(jnp.arange(GRID_W, dtype=F32))
    return rcos, rsin, ccos, csin


def _ret_kernel(*refs, reverse):
    if reverse:
        lg_ref, lgf_ref, q_ref, k_ref, v_ref, s0_ref, o_ref, state, ubuf, ibuf = refs
    else:
        lg_ref, q_ref, k_ref, v_ref, s0_ref, part_ref, gnw_ref, o_ref, state, ubuf = refs
    h = pl.program_id(1)

    @pl.when(pl.program_id(2) == 0)
    def _():
        state[...] = s0_ref[0, 0]

    c = RET_SUB
    lg = lg_ref[h]
    pos = lax.broadcasted_iota(jnp.int32, (c, 1), 0).astype(F32)
    if reverse:
        diff = (lax.broadcasted_iota(jnp.int32, (c, c), 0)
                - lax.broadcasted_iota(jnp.int32, (c, c), 1)).astype(F32)
        dmat = jnp.where(diff < 0, jnp.exp(lg * jnp.maximum(-diff, 0.0)),
                         jnp.exp(lgf_ref[h] * jnp.maximum(diff, 0.0)))
        qdec = jnp.exp(lg * (c - pos))
        kdec = jnp.exp(lg * pos)
    else:
        qdec = jnp.exp(lg * (pos + 1.0))
        kdec = jnp.exp(lg * ((c - 1.0) - pos))
    cdec = jnp.exp(lg * jnp.full((1, 1), float(c), F32))

    nchunk = q_ref.shape[1] // c
    order = list(reversed(range(nchunk))) if reverse else list(range(nchunk))

    def local_terms(ci):
        rows = slice(ci * c, (ci + 1) * c)
        kc = k_ref[0, rows, :]
        vc = v_ref[0, rows, :]
        if reverse:
            s = lax.dot_general(q_ref[0, rows, :], kc, (((1,), (1,)), ((), ())), preferred_element_type=F32)
            ibuf[rows, :] = jnp.dot((s * dmat).astype(BF16), vc, preferred_element_type=F32)
        kd = (kc.astype(F32) * kdec).astype(BF16)
        ubuf[ci] = lax.dot_general(kd, vc, (((0,), (0,)), ((), ())), preferred_element_type=F32)

    local_terms(order[0])
    st = state[...]
    for n, ci in enumerate(order):
        if n + 1 < nchunk:
            local_terms(order[n + 1])
        rows = slice(ci * c, (ci + 1) * c)
        cross = jnp.dot(q_ref[0, rows, :], st.astype(BF16), preferred_element_type=F32) * qdec
        st = st * cdec + ubuf[ci]
        if ci == order[-1]:
            state[...] = st
        if reverse:
            o_ref[0, rows, :] = (ibuf[rows, :] + cross).astype(BF16)
        else:
            o = part_ref[0, rows, :].astype(F32) + cross
            mu = jnp.mean(o, axis=-1, keepdims=True)
            dlt = o - mu
            var = jnp.mean(dlt * dlt, axis=-1, keepdims=True)
            o_ref[0, rows, :] = ((dlt * lax.rsqrt(var + EPS)) * gnw_ref[...]).astype(BF16)


def _retention(lgs, q, k, v, s0, extra, *, reverse, t):
    b, l, _ = q.shape
    nt = l // t
    tmap = (lambda i, h, j: (i, nt - 1 - j, h)) if reverse else (lambda i, h, j: (i, j, h))
    in_specs = [pl.BlockSpec(memory_space=pltpu.SMEM)] * len(lgs)
    in_specs += [pl.BlockSpec((1, t, RET_QK_DIM), tmap),
                 pl.BlockSpec((1, t, RET_QK_DIM), tmap),
                 pl.BlockSpec((1, t, RET_V_DIM), tmap),
                 pl.BlockSpec((1, 1, RET_QK_DIM, RET_V_DIM), lambda i, h, j: (i, h, 0, 0))]
    scratch = [pltpu.VMEM((RET_QK_DIM, RET_V_DIM), F32),
               pltpu.VMEM((t // RET_SUB, RET_QK_DIM, RET_V_DIM), F32)]
    if reverse:
        scratch.append(pltpu.VMEM((t, RET_V_DIM), F32))
    else:
        in_specs += [pl.BlockSpec((1, t, RET_V_DIM), tmap),
                     pl.BlockSpec((1, RET_V_DIM), lambda i, h, j: (0, h))]
    return pl.pallas_call(
        functools.partial(_ret_kernel, reverse=reverse),
        grid=(b, RET_HEADS, nt),
        in_specs=in_specs,
        out_specs=pl.BlockSpec((1, t, RET_V_DIM), tmap),
        out_shape=jax.ShapeDtypeStruct((b, l, RET_V_W), BF16),
        scratch_shapes=scratch,
        compiler_params=_params(("parallel", "parallel", "arbitrary")),
        name="ret_bwd" if reverse else "ret_fwd",
    )(*lgs, q, k, v, s0, *extra)


def _mixer_kernel(on_ref, rg_ref, ys_ref, ga_ref, gb_ref, h_ref, mods_ref, wro_hbm, wco_hbm, wo_hbm, o_ref,
                  wro_ref, wco_ref, wo_ref, stage, sems):
    @pl.when(_first_step())
    def _():
        _load_cast_bf16(wro_hbm, wro_ref, stage, sems)
        _load_cast_bf16(wco_hbm, wco_ref, stage, sems)
        _load_cast_bf16(wo_hbm, wo_ref, stage, sems)

    y_conv = jnp.dot(ys_ref[0], wco_ref[...], preferred_element_type=F32)
    gated = (rg_ref[0].astype(F32) * on_ref[0].astype(F32)).astype(BF16)
    y_ret = jnp.dot(gated, wro_ref[...], preferred_element_type=F32)
    merged = (ga_ref[0].astype(F32) * y_ret + gb_ref[0].astype(F32) * y_conv).astype(BF16)
    m = jnp.dot(merged, wo_ref[...], preferred_element_type=F32)
    g2 = mods_ref[0, 5:6, :]
    o_ref[0] = h_ref[0] + g2 * m


def _mixer(on, srg, ys, sga, sgb, h, mods, wro, wco, wo, *, tm):
    b, l, d = h.shape
    tok = lambda i, t: (i, t, 0)
    hbm = pl.BlockSpec(memory_space=pl.ANY)
    return pl.pallas_call(
        _mixer_kernel,
        grid=(b, l // tm),
        in_specs=[pl.BlockSpec((1, tm, RET_V_W), tok),
                  pl.BlockSpec((1, tm, RET_V_W), tok),
                  pl.BlockSpec((1, tm, d), tok),
                  pl.BlockSpec((1, tm, d), tok),
                  pl.BlockSpec((1, tm, d), tok),
                  pl.BlockSpec((1, tm, d), tok),
                  pl.BlockSpec((1, N_MOD, d), lambda i, t: (i, 0, 0)),
                  hbm, hbm, hbm],
        out_specs=pl.BlockSpec((1, tm, d), tok),
        out_shape=jax.ShapeDtypeStruct((b, l, d), F32),
        scratch_shapes=[pltpu.VMEM(wro.shape, BF16),
                        pltpu.VMEM(wco.shape, BF16),
                        pltpu.VMEM(wo.shape, BF16),
                        pltpu.VMEM((CAST_SLOTS, d // (CAST_CHUNKS // 2), d), F32),
                        pltpu.SemaphoreType.DMA((CAST_SLOTS,))],
        compiler_params=_params(("arbitrary", "arbitrary")),
        name="mixer",
    )(on, srg, ys, sga, sgb, h, mods, wro, wco, wo)


def kernel(x, c, ctx, c_ctx, w_mod, b_mod, norm_ffn1, w_ffn1_in, w_ffn1_out, norm_mix, w_in, ret_decay_f, ret_decay_b, ret_gn_w, w_ret_out, conv_w, conv_b, conv_ln_w, conv_ln_b, w_conv_out, w_out, norm_ffn2, w_ffn2_in, w_ffn2_out, final_norm):
    assert w_mod.shape[0] == 1, "single-layer block"
    b, l, d = x.shape
    assert l % max(FFN_TM, RET_T, MIX_TM) == 0 and d == D_MODEL

    cc = jnp.zeros((SUBLANES, d), F32).at[:b].set(c).at[b].set(c_ctx)
    mods_all = _mods(cc, w_mod[0], b_mod[0])
    mods = mods_all[:b].reshape(b, N_MOD, d)
    mods_c = mods_all[b:b + 1].reshape(1, N_MOD, d)

    lgf = jax.nn.log_sigmoid(ret_decay_f[0].astype(F32))
    lgb = jax.nn.log_sigmoid(ret_decay_b[0].astype(F32))

    h, u = _ffn(x, mods, norm_ffn1[0], w_ffn1_in[0], w_ffn1_out[0], norm_mix[0], row0=0, final=False, tm=FFN_TM)
    _, uc = _ffn(ctx, mods_c, norm_ffn1[0], w_ffn1_in[0], w_ffn1_out[0], norm_mix[0], row0=0, final=False,
                 tm=FFN_TM)

    rf, rb = _ctx_state(uc, w_in[0], lgf, lgb)

    q, k, v, srg, ys, sga, sgb = _inproj(u, w_in[0], _rope_tables(l), conv_w[0], conv_b[0],
                                         conv_ln_w[0], conv_ln_b[0], tm=INPROJ_TM)

    part = _retention((lgb, lgf), q, k, v, rb, (), reverse=True, t=RET_T)
    on = _retention((lgf,), q, k, v, rf, (part, ret_gn_w[0].reshape(1, RET_V_W)), reverse=False, t=RET_T)

    h2 = _mixer(on, srg, ys, sga, sgb, h, mods, w_ret_out[0], w_conv_out[0], w_out[0], tm=MIX_TM)

    return _ffn(h2, mods, norm_ffn2[0], w_ffn2_in[0], w_ffn2_out[0], final_norm, row0=6, final=True, tm=FFN_TM)
```

```python
import functools

import jax
import jax.numpy as jnp
from jax import lax
from jax.experimental import pallas as pl
from jax.experimental.pallas import tpu as pltpu

F32 = jnp.float32
BF16 = jnp.bfloat16

D_MODEL = 1024
GRID_W = 64
RET_HEADS = 4
RET_QK_DIM = 256
RET_V_DIM = 512
CONV_WIDTH = 31
CONV_PAD = CONV_WIDTH // 2
D_FF = 2816
ROPE_BASE = 10000.0
EPS = 1e-6
N_MOD = 9

RET_QK_W = RET_HEADS * RET_QK_DIM
RET_V_W = RET_HEADS * RET_V_DIM
K0 = RET_QK_W
V0 = 2 * RET_QK_W
G0 = V0 + RET_V_W
C0 = G0 + RET_V_W
GA0 = C0 + 2 * D_MODEL
GB0 = GA0 + D_MODEL
IN_WIDTH = GB0 + D_MODEL

V7X_VMEM_LIMIT_BYTES = 56 * 1024 * 1024
SUBLANES = 8
LANES = 128
BF16_ROWS = 16
HALO = 2 * BF16_ROWS

MODS_TN = 2304
CAST_CHUNKS = 16
CAST_SLOTS = 4
FFN_CHUNK = 256
FFN_TM = 1024
INPROJ_TM = 512
INPROJ_TN = 256
MIX_TM = 512
RET_T = 4096
RET_SUB = 256
CONV_RB = 64


def _resident(shape):
    nd = len(shape)
    return pl.BlockSpec(shape, lambda *_: (0,) * nd, pipeline_mode=pl.Buffered(1))


def _params(sem):
    return pltpu.CompilerParams(dimension_semantics=sem, vmem_limit_bytes=V7X_VMEM_LIMIT_BYTES)


def _first_step():
    return (pl.program_id(0) == 0) & (pl.program_id(1) == 0)


def _load_cast_bf16(src_hbm, dst, stage, sems, scaled_cols=None):
    nslot, rows = stage.shape[0], stage.shape[1]
    n = src_hbm.shape[0] // rows
    assert n * rows == src_hbm.shape[0] and n >= nslot

    def copy(i):
        return pltpu.make_async_copy(src_hbm.at[pl.ds(i * rows, rows)], stage.at[i % nslot], sems.at[i % nslot])

    for i in range(nslot - 1):
        copy(i).start()
    for i in range(n):
        if i + nslot - 1 < n:
            copy(i + nslot - 1).start()
        copy(i).wait()
        rs = slice(i * rows, (i + 1) * rows)
        if scaled_cols is None:
            dst[rs, :] = stage[i % nslot].astype(BF16)
        else:
            c0, c1, s = scaled_cols
            dst[rs, :c0] = stage[i % nslot, :, :c0].astype(BF16)
            dst[rs, c0:c1] = (stage[i % nslot, :, c0:c1] * s).astype(BF16)
            dst[rs, c1:] = stage[i % nslot, :, c1:].astype(BF16)


def _rms_mod(x, w, scale, shift):
    y = x * lax.rsqrt(jnp.mean(x * x, axis=-1, keepdims=True) + EPS)
    return (y * w) * (1.0 + scale) + shift


def _layernorm(x, w, b):
    mu = jnp.mean(x, axis=-1, keepdims=True)
    d = x - mu
    var = jnp.mean(d * d, axis=-1, keepdims=True)
    return (d * lax.rsqrt(var + EPS)) * w + b


def _mods_kernel(c_ref, w_ref, b_ref, o_ref):
    c = c_ref[...]
    s = c * jax.nn.sigmoid(c)
    o_ref[...] = jnp.dot(s, w_ref[...], preferred_element_type=F32,
                         precision=lax.Precision.HIGHEST) + b_ref[...]


def _mods(cc, w_mod, b_mod):
    rows, d = cc.shape
    n = w_mod.shape[1]
    tn = MODS_TN
    return pl.pallas_call(
        _mods_kernel,
        grid=(n // tn,),
        in_specs=[pl.BlockSpec((rows, d), lambda j: (0, 0)),
                  pl.BlockSpec((d, tn), lambda j: (0, j)),
                  pl.BlockSpec((1, tn), lambda j: (0, j))],
        out_specs=pl.BlockSpec((rows, tn), lambda j: (0, j)),
        out_shape=jax.ShapeDtypeStruct((rows, n), F32),
        compiler_params=_params(("arbitrary",)),
        name="mods",
    )(cc, w_mod, b_mod.reshape(1, n))


def _ffn_kernel(x_ref, mods_ref, nw_ref, w1_hbm, w2_hbm, n2w_ref, *rest, row0, final):
    n_out = 1 if final else 2
    out_refs = rest[:n_out]
    w1_ref, w2_ref, stage1, stage2, sems = rest[n_out:]

    @pl.when(_first_step())
    def _():
        _load_cast_bf16(w1_hbm, w1_ref, stage1, sems.at[0])
        _load_cast_bf16(w2_hbm, w2_ref, stage2, sems.at[1])

    x = x_ref[0]
    sh = mods_ref[0, row0:row0 + 1, :]
    sc = mods_ref[0, row0 + 1:row0 + 2, :]
    g = mods_ref[0, row0 + 2:row0 + 3, :]
    xm = _rms_mod(x, nw_ref[...], sc, sh).astype(BF16)
    ck = FFN_CHUNK
    acc = jnp.zeros_like(x)
    for c0 in range(0, D_FF, ck):
        a = jnp.dot(xm, w1_ref[:, c0:c0 + ck], preferred_element_type=F32)
        b = jnp.dot(xm, w1_ref[:, D_FF + c0:D_FF + c0 + ck], preferred_element_type=F32)
        act = (a * jax.nn.sigmoid(a) * b).astype(BF16)
        acc = acc + jnp.dot(act, w2_ref[c0:c0 + ck, :], preferred_element_type=F32)
    h = x + (0.5 * g) * acc
    if final:
        (o_ref,) = out_refs
        y = h * lax.rsqrt(jnp.mean(h * h, axis=-1, keepdims=True) + EPS)
        o_ref[0] = y * n2w_ref[...]
    else:
        h_ref, u_ref = out_refs
        h_ref[0] = h
        sh2 = mods_ref[0, row0 + 3:row0 + 4, :]
        sc2 = mods_ref[0, row0 + 4:row0 + 5, :]
        u_ref[0] = _rms_mod(h, n2w_ref[...], sc2, sh2).astype(BF16)


def _ffn(x, mods, norm_w, w1, w2, norm2_w, *, row0, final, tm):
    b, l, d = x.shape
    tm = min(tm, l)
    mods_map = (lambda i, t: (i, 0, 0)) if mods.shape[0] == b else (lambda i, t: (0, 0, 0))
    tok = lambda i, t: (i, t, 0)
    hbm = pl.BlockSpec(memory_space=pl.ANY)
    in_specs = [pl.BlockSpec((1, tm, d), tok),
                pl.BlockSpec((1, N_MOD, d), mods_map),
                _resident((1, d)),
                hbm,
                hbm,
                _resident((1, d))]
    scratch = [pltpu.VMEM(w1.shape, BF16),
               pltpu.VMEM(w2.shape, BF16),
               pltpu.VMEM((CAST_SLOTS, w1.shape[0] // CAST_CHUNKS, w1.shape[1]), F32),
               pltpu.VMEM((CAST_SLOTS, w2.shape[0] // CAST_CHUNKS, w2.shape[1]), F32),
               pltpu.SemaphoreType.DMA((2, CAST_SLOTS))]
    if final:
        out_specs = pl.BlockSpec((1, tm, d), tok)
        out_shape = jax.ShapeDtypeStruct((b, l, d), F32)
    else:
        out_specs = [pl.BlockSpec((1, tm, d), tok), pl.BlockSpec((1, tm, d), tok)]
        out_shape = [jax.ShapeDtypeStruct((b, l, d), F32), jax.ShapeDtypeStruct((b, l, d), BF16)]
    return pl.pallas_call(
        functools.partial(_ffn_kernel, row0=row0, final=final),
        grid=(b, l // tm),
        in_specs=in_specs,
        out_specs=out_specs,
        out_shape=out_shape,
        scratch_shapes=scratch,
        compiler_params=_params(("arbitrary", "arbitrary")),
        name="ffn_final" if final else "ffn_mix",
    )(x, mods, norm_w.reshape(1, d), w1, w2, norm2_w.reshape(1, d))


def _ctx_state_kernel(lgf_ref, lgb_ref, u_ref, wk_ref, wv_ref, rf_ref, rb_ref):
    h = pl.program_id(1)
    u = u_ref[0]
    n = u.shape[0]
    k = jnp.dot(u, wk_ref[...].astype(BF16), preferred_element_type=F32) * (RET_QK_DIM ** -0.5)
    v = jnp.dot(u, wv_ref[...].astype(BF16), preferred_element_type=F32).astype(BF16)
    pos = lax.broadcasted_iota(jnp.int32, (n, 1), 0).astype(F32)
    wf = jnp.exp(lgf_ref[h] * ((n - 1.0) - pos))
    wb = jnp.exp(lgb_ref[h] * pos)
    tdot = lambda a, c: lax.dot_general(a, c, (((0,), (0,)), ((), ())), preferred_element_type=F32)
    rf_ref[0, 0] = tdot((k * wf).astype(BF16), v)
    rb_ref[0, 0] = tdot((k * wb).astype(BF16), v)


def _ctx_state(uc, w_in, lgf, lgb):
    b, n, d = uc.shape
    smem = pl.BlockSpec(memory_space=pltpu.SMEM)
    st = jax.ShapeDtypeStruct((b, RET_HEADS, RET_QK_DIM, RET_V_DIM), F32)
    st_spec = pl.BlockSpec((1, 1, RET_QK_DIM, RET_V_DIM), lambda i, h: (i, h, 0, 0))
    return pl.pallas_call(
        _ctx_state_kernel,
        grid=(b, RET_HEADS),
        in_specs=[smem, smem,
                  pl.BlockSpec((1, n, d), lambda i, h: (i, 0, 0)),
                  pl.BlockSpec((d, RET_QK_DIM), lambda i, h: (0, K0 // RET_QK_DIM + h)),
                  pl.BlockSpec((d, RET_V_DIM), lambda i, h: (0, V0 // RET_V_DIM + h))],
        out_specs=[st_spec, st_spec],
        out_shape=[st, st],
        compiler_params=_params(("parallel", "parallel")),
        name="ctx_state",
    )(lgf, lgb, uc, w_in, w_in)


def _depthwise_conv(ybuf, cw_ref, cbuf, c0, r0, zero):
    rows = CONV_RB + HALO
    base = BF16_ROWS - CONV_PAD
    cs = slice(c0, c0 + LANES)
    src = ybuf[r0:r0 + rows, cs]
    parts = [None] * (HALO // SUBLANES)
    for r in range(SUBLANES):
        zr = src if r == 0 else pltpu.roll(src, rows - r, 0)
        for ai, a0 in enumerate(range(0, HALO, SUBLANES)):
            j = a0 + r - base
            if 0 <= j < CONV_WIDTH:
                term = zr[a0:a0 + CONV_RB, :] * cw_ref[j:j + 1, cs]
                parts[ai] = term if parts[ai] is None else parts[ai] + term
    acc = (parts[0] + parts[1]) + (parts[2] + parts[3])
    cbuf[r0:r0 + CONV_RB, cs] = acc + jnp.tile(zero, (CONV_RB // SUBLANES, 1))


def _inproj_kernel(u_ref, uprev_ref, unext_ref, w_hbm, rcos_ref, rsin_ref, ccos_ref, csin_ref,
                   cw_ref, cb_ref, lnw_ref, lnb_ref,
                   q_ref, k_ref, v_ref, rg_ref, ys_ref, ga_ref, gb_ref,
                   uext, ybuf, cbuf, w_ref, stage, sems):
    @pl.when(_first_step())
    def _():
        _load_cast_bf16(w_hbm, w_ref, stage, sems, scaled_cols=(K0, V0, RET_QK_DIM ** -0.5))

    t = pl.program_id(1)
    tm = u_ref.shape[1]
    hb = BF16_ROWS
    u = u_ref[0]

    def proj(c0):
        return jnp.dot(u, w_ref[:, c0:c0 + INPROJ_TN], preferred_element_type=F32)

    def rope(val, ref, o0):
        for gi in range(tm // GRID_W):
            rs = slice(gi * GRID_W, (gi + 1) * GRID_W)
            for l0 in range(0, INPROJ_TN, LANES):
                if (o0 + l0) % RET_QK_DIM == 0:
                    cs, sn = rcos_ref[0, gi:gi + 1, :], rsin_ref[0, gi:gi + 1, :]
                else:
                    cs, sn = ccos_ref[...], csin_ref[...]
                slab = val[rs, l0:l0 + LANES]
                ref[0, rs, o0 + l0:o0 + l0 + LANES] = (
                    slab * cs + pltpu.roll(slab, LANES // 2, 1) * sn).astype(BF16)

    def piece(c0):
        val = proj(c0)
        if c0 < K0:
            rope(val, q_ref, c0)
        elif c0 < V0:
            rope(val, k_ref, c0 - K0)
        elif c0 < G0:
            v_ref[0, :, c0 - V0:c0 - V0 + INPROJ_TN] = val.astype(BF16)
        elif c0 < C0:
            rg_ref[0, :, c0 - G0:c0 - G0 + INPROJ_TN] = (val * jax.nn.sigmoid(val)).astype(BF16)
        elif c0 < GB0:
            ga_ref[0, :, c0 - GA0:c0 - GA0 + INPROJ_TN] = jax.nn.sigmoid(val).astype(BF16)
        else:
            gb_ref[0, :, c0 - GB0:c0 - GB0 + INPROJ_TN] = jax.nn.sigmoid(val).astype(BF16)
        return val

    uext[0:hb, :] = jnp.where(t > 0, uprev_ref[0], jnp.zeros_like(uprev_ref[0]))
    uext[hb:hb + tm, :] = u
    uext[hb + tm:, :] = jnp.where(t < pl.num_programs(1) - 1, unext_ref[0], jnp.zeros_like(unext_ref[0]))
    ue = uext[...]
    ca = jnp.dot(ue, w_ref[:, C0:C0 + D_MODEL], preferred_element_type=F32)
    cg = jnp.dot(ue, w_ref[:, C0 + D_MODEL:GA0], preferred_element_type=F32)
    ybuf[...] = ca * jax.nn.sigmoid(cg)

    cols = [c0 for c0 in range(0, IN_WIDTH, INPROJ_TN) if not C0 <= c0 < GA0]
    blocks = [(c0, r0) for c0 in range(0, D_MODEL, LANES) for r0 in range(0, tm, CONV_RB)]
    per_tile = len(blocks) // len(cols)
    assert per_tile * len(cols) == len(blocks)
    zero = jnp.zeros((SUBLANES, LANES), F32)
    for i, c0 in enumerate(cols):
        for b0, r0 in blocks[i * per_tile:(i + 1) * per_tile]:
            _depthwise_conv(ybuf, cw_ref, cbuf, b0, r0, zero)
        bits = piece(c0)[0:SUBLANES, 0:LANES].astype(jnp.int32)
        zero = lax.shift_right_logical(lax.shift_right_logical(bits, 16), 16).astype(F32)

    yn = _layernorm(cbuf[...] + cb_ref[...], lnw_ref[...], lnb_ref[...])
    ys_ref[0] = (yn * jax.nn.sigmoid(yn)).astype(BF16)


def _inproj(u, w_in, rope, conv_w, conv_b, ln_w, ln_b, *, tm):
    b, l, d = u.shape
    hb = BF16_ROWS
    gpt = tm // GRID_W
    rcos, rsin, ccos, csin = rope
    tok = lambda i, t: (i, t, 0)
    prev = lambda i, t: (i, jnp.maximum(t * (tm // hb) - 1, 0), 0)
    nxt = lambda i, t: (i, jnp.minimum((t + 1) * (tm // hb), l // hb - 1), 0)
    rtab = lambda i, t: (t, 0, 0)
    vec = lambda a: a.reshape(1, d)
    widths = (RET_QK_W, RET_QK_W, RET_V_W, RET_V_W, D_MODEL, D_MODEL, D_MODEL)
    return pl.pallas_call(
        _inproj_kernel,
        grid=(b, l // tm),
        in_specs=[pl.BlockSpec((1, tm, d), tok),
                  pl.BlockSpec((1, hb, d), prev),
                  pl.BlockSpec((1, hb, d), nxt),
                  pl.BlockSpec(memory_space=pl.ANY),
                  pl.BlockSpec((1, gpt, LANES), rtab),
                  pl.BlockSpec((1, gpt, LANES), rtab),
                  _resident(ccos.shape),
                  _resident(csin.shape),
                  _resident(conv_w.shape),
                  _resident((1, d)), _resident((1, d)), _resident((1, d))],
        out_specs=[pl.BlockSpec((1, tm, w), tok) for w in widths],
        out_shape=[jax.ShapeDtypeStruct((b, l, w), BF16) for w in widths],
        scratch_shapes=[pltpu.VMEM((tm + HALO, d), BF16),
                        pltpu.VMEM((tm + HALO, d), F32),
                        pltpu.VMEM((tm, d), F32),
                        pltpu.VMEM(w_in.shape, BF16),
                        pltpu.VMEM((CAST_SLOTS, d // (2 * CAST_CHUNKS), IN_WIDTH), F32),
                        pltpu.SemaphoreType.DMA((CAST_SLOTS,))],
        compiler_params=_params(("arbitrary", "arbitrary")),
        name="inproj",
    )(u, u, u, w_in, rcos.reshape(l // tm, gpt, LANES), rsin.reshape(l // tm, gpt, LANES),
      ccos, csin, conv_w, vec(conv_b), vec(ln_w), vec(ln_b))


def _rope_tables(l):
    half = RET_QK_DIM // 2
    inv = ROPE_BASE ** (-jnp.arange(0, half, 2, dtype=F32) / half)

    def tabs(pos):
        ang = pos[:, None] * inv[None, :]
        return (jnp.concatenate([jnp.cos(ang), jnp.cos(ang)], axis=-1),
                jnp.concatenate([-jnp.sin(ang), jnp.sin(ang)], axis=-1))

    rcos, rsin = tabs(jnp.arange(l // GRID_W, dtype=F32))
    ccos, csin = tabs(jnp.arange(GRID_W, dtype=F32))
    return rcos, rsin, ccos, csin


def _ret_kernel(*refs, reverse):
    if reverse:
        lg_ref, lgf_ref, q_ref, k_ref, v_ref, s0_ref, o_ref, state, ubuf, ibuf = refs
    else:
        lg_ref, q_ref, k_ref, v_ref, s0_ref, part_ref, gnw_ref, o_ref, state, ubuf = refs
    h = pl.program_id(1)

    @pl.when(pl.program_id(2) == 0)
    def _():
        state[...] = s0_ref[0, 0]

    c = RET_SUB
    lg = lg_ref[h]
    pos = lax.broadcasted_iota(jnp.int32, (c, 1), 0).astype(F32)
    if reverse:
        diff = (lax.broadcasted_iota(jnp.int32, (c, c), 0)
                - lax.broadcasted_iota(jnp.int32, (c, c), 1)).astype(F32)
        dmat = jnp.where(diff < 0, jnp.exp(lg * jnp.maximum(-diff, 0.0)),
                         jnp.exp(lgf_ref[h] * jnp.maximum(diff, 0.0)))
        qdec = jnp.exp(lg * (c - pos))
        kdec = jnp.exp(lg * pos)
    else:
        qdec = jnp.exp(lg * (pos + 1.0))
        kdec = jnp.exp(lg * ((c - 1.0) - pos))
    cdec = jnp.exp(lg * jnp.full((1, 1), float(c), F32))

    nchunk = q_ref.shape[1] // c
    order = list(reversed(range(nchunk))) if reverse else list(range(nchunk))

    def local_terms(ci):
        rows = slice(ci * c, (ci + 1) * c)
        kc = k_ref[0, rows, :]
        vc = v_ref[0, rows, :]
        if reverse:
            s = lax.dot_general(q_ref[0, rows, :], kc, (((1,), (1,)), ((), ())), preferred_element_type=F32)
            ibuf[rows, :] = jnp.dot((s * dmat).astype(BF16), vc, preferred_element_type=F32)
        kd = (kc.astype(F32) * kdec).astype(BF16)
        ubuf[ci] = lax.dot_general(kd, vc, (((0,), (0,)), ((), ())), preferred_element_type=F32)

    local_terms(order[0])
    st = state[...]
    for n, ci in enumerate(order):
        if n + 1 < nchunk:
            local_terms(order[n + 1])
        rows = slice(ci * c, (ci + 1) * c)
        cross = jnp.dot(q_ref[0, rows, :], st.astype(BF16), preferred_element_type=F32) * qdec
        st = st * cdec + ubuf[ci]
        if ci == order[-1]:
            state[...] = st
        if reverse:
            o_ref[0, rows, :] = (ibuf[rows, :] + cross).astype(BF16)
        else:
            o = part_ref[0, rows, :].astype(F32) + cross
            mu = jnp.mean(o, axis=-1, keepdims=True)
            dlt = o - mu
            var = jnp.mean(dlt * dlt, axis=-1, keepdims=True)
            o_ref[0, rows, :] = ((dlt * lax.rsqrt(var + EPS)) * gnw_ref[...]).astype(BF16)


def _retention(lgs, q, k, v, s0, extra, *, reverse, t):
    b, l, _ = q.shape
    nt = l // t
    tmap = (lambda i, h, j: (i, nt - 1 - j, h)) if reverse else (lambda i, h, j: (i, j, h))
    in_specs = [pl.BlockSpec(memory_space=pltpu.SMEM)] * len(lgs)
    in_specs += [pl.BlockSpec((1, t, RET_QK_DIM), tmap),
                 pl.BlockSpec((1, t, RET_QK_DIM), tmap),
                 pl.BlockSpec((1, t, RET_V_DIM), tmap),
                 pl.BlockSpec((1, 1, RET_QK_DIM, RET_V_DIM), lambda i, h, j: (i, h, 0, 0))]
    scratch = [pltpu.VMEM((RET_QK_DIM, RET_V_DIM), F32),
               pltpu.VMEM((t // RET_SUB, RET_QK_DIM, RET_V_DIM), F32)]
    if reverse:
        scratch.append(pltpu.VMEM((t, RET_V_DIM), F32))
    else:
        in_specs += [pl.BlockSpec((1, t, RET_V_DIM), tmap),
                     pl.BlockSpec((1, RET_V_DIM), lambda i, h, j: (0, h))]
    return pl.pallas_call(
        functools.partial(_ret_kernel, reverse=reverse),
        grid=(b, RET_HEADS, nt),
        in_specs=in_specs,
        out_specs=pl.BlockSpec((1, t, RET_V_DIM), tmap),
        out_shape=jax.ShapeDtypeStruct((b, l, RET_V_W), BF16),
        scratch_shapes=scratch,
        compiler_params=_params(("parallel", "parallel", "arbitrary")),
        name="ret_bwd" if reverse else "ret_fwd",
    )(*lgs, q, k, v, s0, *extra)


def _mixer_kernel(on_ref, rg_ref, ys_ref, ga_ref, gb_ref, h_ref, mods_ref, wro_hbm, wco_hbm, wo_hbm, o_ref,
                  wro_ref, wco_ref, wo_ref, stage, sems):
    @pl.when(_first_step())
    def _():
        _load_cast_bf16(wro_hbm, wro_ref, stage, sems)
        _load_cast_bf16(wco_hbm, wco_ref, stage, sems)
        _load_cast_bf16(wo_hbm, wo_ref, stage, sems)

    y_conv = jnp.dot(ys_ref[0], wco_ref[...], preferred_element_type=F32)
    gated = (rg_ref[0].astype(F32) * on_ref[0].astype(F32)).astype(BF16)
    y_ret = jnp.dot(gated, wro_ref[...], preferred_element_type=F32)
    merged = (ga_ref[0].astype(F32) * y_ret + gb_ref[0].astype(F32) * y_conv).astype(BF16)
    m = jnp.dot(merged, wo_ref[...], preferred_element_type=F32)
    g2 = mods_ref[0, 5:6, :]
    o_ref[0] = h_ref[0] + g2 * m


def _mixer(on, srg, ys, sga, sgb, h, mods, wro, wco, wo, *, tm):
    b, l, d = h.shape
    tok = lambda i, t: (i, t, 0)
    hbm = pl.BlockSpec(memory_space=pl.ANY)
    return pl.pallas_call(
        _mixer_kernel,
        grid=(b, l // tm),
        in_specs=[pl.BlockSpec((1, tm, RET_V_W), tok),
                  pl.BlockSpec((1, tm, RET_V_W), tok),
                  pl.BlockSpec((1, tm, d), tok),
                  pl.BlockSpec((1, tm, d), tok),
                  pl.BlockSpec((1, tm, d), tok),
                  pl.BlockSpec((1, tm, d), tok),
                  pl.BlockSpec((1, N_MOD, d), lambda i, t: (i, 0, 0)),
                  hbm, hbm, hbm],
        out_specs=pl.BlockSpec((1, tm, d), tok),
        out_shape=jax.ShapeDtypeStruct((b, l, d), F32),
        scratch_shapes=[pltpu.VMEM(wro.shape, BF16),
                        pltpu.VMEM(wco.shape, BF16),
                        pltpu.VMEM(wo.shape, BF16),
                        pltpu.VMEM((CAST_SLOTS, d // (CAST_CHUNKS // 2), d), F32),
                        pltpu.SemaphoreType.DMA((CAST_SLOTS,))],
        compiler_params=_params(("arbitrary", "arbitrary")),
        name="mixer",
    )(on, srg, ys, sga, sgb, h, mods, wro, wco, wo)


def kernel(x, c, ctx, c_ctx, w_mod, b_mod, norm_ffn1, w_ffn1_in, w_ffn1_out, norm_mix, w_in, ret_decay_f, ret_decay_b, ret_gn_w, w_ret_out, conv_w, conv_b, conv_ln_w, conv_ln_b, w_conv_out, w_out, norm_ffn2, w_ffn2_in, w_ffn2_out, final_norm):
    assert w_mod.shape[0] == 1, "single-layer block"
    b, l, d = x.shape
    assert l % max(FFN_TM, RET_T, MIX_TM) == 0 and d == D_MODEL

    cc = jnp.zeros((SUBLANES, d), F32).at[:b].set(c).at[b].set(c_ctx)
    mods_all = _mods(cc, w_mod[0], b_mod[0])
    mods = mods_all[:b].reshape(b, N_MOD, d)
    mods_c = mods_all[b:b + 1].reshape(1, N_MOD, d)

    lgf = jax.nn.log_sigmoid(ret_decay_f[0].astype(F32))
    lgb = jax.nn.log_sigmoid(ret_decay_b[0].astype(F32))

    h, u = _ffn(x, mods, norm_ffn1[0], w_ffn1_in[0], w_ffn1_out[0], norm_mix[0], row0=0, final=False, tm=FFN_TM)
    _, uc = _ffn(ctx, mods_c, norm_ffn1[0], w_ffn1_in[0], w_ffn1_out[0], norm_mix[0], row0=0, final=False,
                 tm=FFN_TM)

    rf, rb = _ctx_state(uc, w_in[0], lgf, lgb)

    q, k, v, srg, ys, sga, sgb = _inproj(u, w_in[0], _rope_tables(l), conv_w[0], conv_b[0],
                                         conv_ln_w[0], conv_ln_b[0], tm=INPROJ_TM)

    part = _retention((lgb, lgf), q, k, v, rb, (), reverse=True, t=RET_T)
    on = _retention((lgf,), q, k, v, rf, (part, ret_gn_w[0].reshape(1, RET_V_W)), reverse=False, t=RET_T)

    h2 = _mixer(on, srg, ys, sga, sgb, h, mods, w_ret_out[0], w_conv_out[0], w_out[0], tm=MIX_TM)

    return _ffn(h2, mods, norm_ffn2[0], w_ffn2_in[0], w_ffn2_out[0], final_norm, row0=6, final=True, tm=FFN_TM)
```

```python
import functools

import jax
import jax.numpy as jnp
from jax import lax
from jax.experimental import pallas as pl
from jax.experimental.pallas import tpu as pltpu

F32 = jnp.float32
BF16 = jnp.bfloat16

D_MODEL = 1024
GRID_W = 64
RET_HEADS = 4
RET_QK_DIM = 256
RET_V_DIM = 512
CONV_WIDTH = 31
CONV_PAD = CONV_WIDTH // 2
D_FF = 2816
ROPE_BASE = 10000.0
EPS = 1e-6
N_MOD = 9

RET_QK_W = RET_HEADS * RET_QK_DIM
RET_V_W = RET_HEADS * RET_V_DIM
K0 = RET_QK_W
V0 = 2 * RET_QK_W
G0 = V0 + RET_V_W
C0 = G0 + RET_V_W
GA0 = C0 + 2 * D_MODEL
GB0 = GA0 + D_MODEL
IN_WIDTH = GB0 + D_MODEL

V7X_VMEM_LIMIT_BYTES = 56 * 1024 * 1024
SUBLANES = 8
LANES = 128
BF16_ROWS = 16
HALO = 2 * BF16_ROWS

MODS_TN = 2304
CAST_CHUNKS = 16
CAST_SLOTS = 4
FFN_CHUNK = 256
FFN_TM = 1024
INPROJ_TM = 512
INPROJ_TN = 256
MIX_TM = 512
RET_T = 4096
RET_SUB = 256
CONV_RB = 64


def _resident(shape):
    nd = len(shape)
    return pl.BlockSpec(shape, lambda *_: (0,) * nd, pipeline_mode=pl.Buffered(1))


def _params(sem):
    return pltpu.CompilerParams(dimension_semantics=sem, vmem_limit_bytes=V7X_VMEM_LIMIT_BYTES)


def _first_step():
    return (pl.program_id(0) == 0) & (pl.program_id(1) == 0)


def _load_cast_bf16(src_hbm, dst, stage, sems, scaled_cols=None):
    nslot, rows = stage.shape[0], stage.shape[1]
    n = src_hbm.shape[0] // rows
    assert n * rows == src_hbm.shape[0] and n >= nslot

    def copy(i):
        return pltpu.make_async_copy(src_hbm.at[pl.ds(i * rows, rows)], stage.at[i % nslot], sems.at[i % nslot])

    for i in range(nslot - 1):
        copy(i).start()
    for i in range(n):
        if i + nslot - 1 < n:
            copy(i + nslot - 1).start()
        copy(i).wait()
        rs = slice(i * rows, (i + 1) * rows)
        if scaled_cols is None:
            dst[rs, :] = stage[i % nslot].astype(BF16)
        else:
            c0, c1, s = scaled_cols
            dst[rs, :c0] = stage[i % nslot, :, :c0].astype(BF16)
            dst[rs, c0:c1] = (stage[i % nslot, :, c0:c1] * s).astype(BF16)
            dst[rs, c1:] = stage[i % nslot, :, c1:].astype(BF16)


def _rms_mod(x, w, scale, shift):
    y = x * lax.rsqrt(jnp.mean(x * x, axis=-1, keepdims=True) + EPS)
    return (y * w) * (1.0 + scale) + shift


def _layernorm(x, w, b):
    mu = jnp.mean(x, axis=-1, keepdims=True)
    d = x - mu
    var = jnp.mean(d * d, axis=-1, keepdims=True)
    return (d * lax.rsqrt(var + EPS)) * w + b


def _split_bf16(a):
    hi = a.astype(BF16)
    return hi, (a - hi.astype(F32)).astype(BF16)


def _mods_kernel(c_ref, w_ref, b_ref, o_ref):
    c = c_ref[...]
    s_hi, s_lo = _split_bf16(c * jax.nn.sigmoid(c))
    w_hi, w_lo = _split_bf16(w_ref[...])
    dot = functools.partial(jnp.dot, preferred_element_type=F32)
    o_ref[...] = dot(s_hi, w_hi) + (dot(s_hi, w_lo) + dot(s_lo, w_hi)) + b_ref[...]


def _mods(cc, w_mod, b_mod):
    rows, d = cc.shape
    n = w_mod.shape[1]
    tn = MODS_TN
    return pl.pallas_call(
        _mods_kernel,
        grid=(n // tn,),
        in_specs=[pl.BlockSpec((rows, d), lambda j: (0, 0)),
                  pl.BlockSpec((d, tn), lambda j: (0, j)),
                  pl.BlockSpec((1, tn), lambda j: (0, j))],
        out_specs=pl.BlockSpec((rows, tn), lambda j: (0, j)),
        out_shape=jax.ShapeDtypeStruct((rows, n), F32),
        compiler_params=_params(("arbitrary",)),
        name="mods",
    )(cc, w_mod, b_mod.reshape(1, n))


def _ffn_kernel(x_ref, mods_ref, nw_ref, w1_hbm, w2_hbm, n2w_ref, *rest, row0, final):
    n_out = 1 if final else 2
    out_refs = rest[:n_out]
    w1_ref, w2_ref, stage1, stage2, sems = rest[n_out:]

    @pl.when(_first_step())
    def _():
        _load_cast_bf16(w1_hbm, w1_ref, stage1, sems.at[0])
        _load_cast_bf16(w2_hbm, w2_ref, stage2, sems.at[1])

    x = x_ref[0]
    sh = mods_ref[0, row0:row0 + 1, :]
    sc = mods_ref[0, row0 + 1:row0 + 2, :]
    g = mods_ref[0, row0 + 2:row0 + 3, :]
    xm = _rms_mod(x, nw_ref[...], sc, sh).astype(BF16)
    ck = FFN_CHUNK
    acc = jnp.zeros_like(x)
    for c0 in range(0, D_FF, ck):
        a = jnp.dot(xm, w1_ref[:, c0:c0 + ck], preferred_element_type=F32)
        b = jnp.dot(xm, w1_ref[:, D_FF + c0:D_FF + c0 + ck], preferred_element_type=F32)
        act = (a * jax.nn.sigmoid(a) * b).astype(BF16)
        acc = acc + jnp.dot(act, w2_ref[c0:c0 + ck, :], preferred_element_type=F32)
    h = x + (0.5 * g) * acc
    if final:
        (o_ref,) = out_refs
        y = h * lax.rsqrt(jnp.mean(h * h, axis=-1, keepdims=True) + EPS)
        o_ref[0] = y * n2w_ref[...]
    else:
        h_ref, u_ref = out_refs
        h_ref[0] = h
        sh2 = mods_ref[0, row0 + 3:row0 + 4, :]
        sc2 = mods_ref[0, row0 + 4:row0 + 5, :]
        u_ref[0] = _rms_mod(h, n2w_ref[...], sc2, sh2).astype(BF16)


def _ffn(x, mods, norm_w, w1, w2, norm2_w, *, row0, final, tm):
    b, l, d = x.shape
    tm = min(tm, l)
    mods_map = (lambda i, t: (i, 0, 0)) if mods.shape[0] == b else (lambda i, t: (0, 0, 0))
    tok = lambda i, t: (i, t, 0)
    hbm = pl.BlockSpec(memory_space=pl.ANY)
    in_specs = [pl.BlockSpec((1, tm, d), tok),
                pl.BlockSpec((1, N_MOD, d), mods_map),
                _resident((1, d)),
                hbm,
                hbm,
                _resident((1, d))]
    scratch = [pltpu.VMEM(w1.shape, BF16),
               pltpu.VMEM(w2.shape, BF16),
               pltpu.VMEM((CAST_SLOTS, w1.shape[0] // CAST_CHUNKS, w1.shape[1]), F32),
               pltpu.VMEM((CAST_SLOTS, w2.shape[0] // CAST_CHUNKS, w2.shape[1]), F32),
               pltpu.SemaphoreType.DMA((2, CAST_SLOTS))]
    if final:
        out_specs = pl.BlockSpec((1, tm, d), tok)
        out_shape = jax.ShapeDtypeStruct((b, l, d), F32)
    else:
        out_specs = [pl.BlockSpec((1, tm, d), tok), pl.BlockSpec((1, tm, d), tok)]
        out_shape = [jax.ShapeDtypeStruct((b, l, d), F32), jax.ShapeDtypeStruct((b, l, d), BF16)]
    return pl.pallas_call(
        functools.partial(_ffn_kernel, row0=row0, final=final),
        grid=(b, l // tm),
        in_specs=in_specs,
        out_specs=out_specs,
        out_shape=out_shape,
        scratch_shapes=scratch,
        compiler_params=_params(("arbitrary", "arbitrary")),
        name="ffn_final" if final else "ffn_mix",
    )(x, mods, norm_w.reshape(1, d), w1, w2, norm2_w.reshape(1, d))


def _ctx_state_kernel(lgf_ref, lgb_ref, u_ref, wk_ref, wv_ref, rf_ref, rb_ref):
    h = pl.program_id(1)
    u = u_ref[0]
    n = u.shape[0]
    k = jnp.dot(u, wk_ref[...].astype(BF16), preferred_element_type=F32) * (RET_QK_DIM ** -0.5)
    v = jnp.dot(u, wv_ref[...].astype(BF16), preferred_element_type=F32).astype(BF16)
    pos = lax.broadcasted_iota(jnp.int32, (n, 1), 0).astype(F32)
    wf = jnp.exp(lgf_ref[h] * ((n - 1.0) - pos))
    wb = jnp.exp(lgb_ref[h] * pos)
    tdot = lambda a, c: lax.dot_general(a, c, (((0,), (0,)), ((), ())), preferred_element_type=F32)
    rf_ref[0, 0] = tdot((k * wf).astype(BF16), v)
    rb_ref[0, 0] = tdot((k * wb).astype(BF16), v)


def _ctx_state(uc, w_in, lgf, lgb):
    b, n, d = uc.shape
    smem = pl.BlockSpec(memory_space=pltpu.SMEM)
    st = jax.ShapeDtypeStruct((b, RET_HEADS, RET_QK_DIM, RET_V_DIM), F32)
    st_spec = pl.BlockSpec((1, 1, RET_QK_DIM, RET_V_DIM), lambda i, h: (i, h, 0, 0))
    return pl.pallas_call(
        _ctx_state_kernel,
        grid=(b, RET_HEADS),
        in_specs=[smem, smem,
                  pl.BlockSpec((1, n, d), lambda i, h: (i, 0, 0)),
                  pl.BlockSpec((d, RET_QK_DIM), lambda i, h: (0, K0 // RET_QK_DIM + h)),
                  pl.BlockSpec((d, RET_V_DIM), lambda i, h: (0, V0 // RET_V_DIM + h))],
        out_specs=[st_spec, st_spec],
        out_shape=[st, st],
        compiler_params=_params(("parallel", "parallel")),
        name="ctx_state",
    )(lgf, lgb, uc, w_in, w_in)


def _depthwise_conv(ybuf, cw_ref, cbuf, c0, r0, zero):
    rows = CONV_RB + HALO
    base = BF16_ROWS - CONV_PAD
    cs = slice(c0, c0 + LANES)
    src = ybuf[r0:r0 + rows, cs]
    parts = [None] * (HALO // SUBLANES)
    for r in range(SUBLANES):
        zr = src if r == 0 else pltpu.roll(src, rows - r, 0)
        for ai, a0 in enumerate(range(0, HALO, SUBLANES)):
            j = a0 + r - base
            if 0 <= j < CONV_WIDTH:
                term = zr[a0:a0 + CONV_RB, :] * cw_ref[j:j + 1, cs]
                parts[ai] = term if parts[ai] is None else parts[ai] + term
    acc = (parts[0] + parts[1]) + (parts[2] + parts[3])
    cbuf[r0:r0 + CONV_RB, cs] = acc + jnp.tile(zero, (CONV_RB // SUBLANES, 1))


def _inproj_kernel(u_ref, uprev_ref, unext_ref, w_hbm, rcos_ref, rsin_ref, ccos_ref, csin_ref,
                   cw_ref, cb_ref, lnw_ref, lnb_ref,
                   q_ref, k_ref, v_ref, rg_ref, ys_ref, ga_ref, gb_ref,
                   uext, ybuf, cbuf, w_ref, stage, sems):
    @pl.when(_first_step())
    def _():
        _load_cast_bf16(w_hbm, w_ref, stage, sems, scaled_cols=(K0, V0, RET_QK_DIM ** -0.5))

    t = pl.program_id(1)
    tm = u_ref.shape[1]
    hb = BF16_ROWS
    u = u_ref[0]

    def proj(c0):
        return jnp.dot(u, w_ref[:, c0:c0 + INPROJ_TN], preferred_element_type=F32)

    def rope(val, ref, o0):
        for gi in range(tm // GRID_W):
            rs = slice(gi * GRID_W, (gi + 1) * GRID_W)
            for l0 in range(0, INPROJ_TN, LANES):
                if (o0 + l0) % RET_QK_DIM == 0:
                    cs, sn = rcos_ref[0, gi:gi + 1, :], rsin_ref[0, gi:gi + 1, :]
                else:
                    cs, sn = ccos_ref[...], csin_ref[...]
                slab = val[rs, l0:l0 + LANES]
                ref[0, rs, o0 + l0:o0 + l0 + LANES] = (
                    slab * cs + pltpu.roll(slab, LANES // 2, 1) * sn).astype(BF16)

    def piece(c0):
        val = proj(c0)
        if c0 < K0:
            rope(val, q_ref, c0)
        elif c0 < V0:
            rope(val, k_ref, c0 - K0)
        elif c0 < G0:
            v_ref[0, :, c0 - V0:c0 - V0 + INPROJ_TN] = val.astype(BF16)
        elif c0 < C0:
            rg_ref[0, :, c0 - G0:c0 - G0 + INPROJ_TN] = (val * jax.nn.sigmoid(val)).astype(BF16)
        elif c0 < GB0:
            ga_ref[0, :, c0 - GA0:c0 - GA0 + INPROJ_TN] = jax.nn.sigmoid(val).astype(BF16)
        else:
            gb_ref[0, :, c0 - GB0:c0 - GB0 + INPROJ_TN] = jax.nn.sigmoid(val).astype(BF16)
        return val

    uext[0:hb, :] = jnp.where(t > 0, uprev_ref[0], jnp.zeros_like(uprev_ref[0]))
    uext[hb:hb + tm, :] = u
    uext[hb + tm:, :] = jnp.where(t < pl.num_programs(1) - 1, unext_ref[0], jnp.zeros_like(unext_ref[0]))
    ue = uext[...]
    ca = jnp.dot(ue, w_ref[:, C0:C0 + D_MODEL], preferred_element_type=F32)
    cg = jnp.dot(ue, w_ref[:, C0 + D_MODEL:GA0], preferred_element_type=F32)
    ybuf[...] = ca * jax.nn.sigmoid(cg)

    cols = [c0 for c0 in range(0, IN_WIDTH, INPROJ_TN) if not C0 <= c0 < GA0]
    blocks = [(c0, r0) for c0 in range(0, D_MODEL, LANES) for r0 in range(0, tm, CONV_RB)]
    per_tile = len(blocks) // len(cols)
    assert per_tile * len(cols) == len(blocks)
    zero = jnp.zeros((SUBLANES, LANES), F32)
    for i, c0 in enumerate(cols):
        for b0, r0 in blocks[i * per_tile:(i + 1) * per_tile]:
            _depthwise_conv(ybuf, cw_ref, cbuf, b0, r0, zero)
        bits = piece(c0)[0:SUBLANES, 0:LANES].astype(jnp.int32)
        zero = lax.shift_right_logical(lax.shift_right_logical(bits, 16), 16).astype(F32)

    yn = _layernorm(cbuf[...] + cb_ref[...], lnw_ref[...], lnb_ref[...])
    ys_ref[0] = (yn * jax.nn.sigmoid(yn)).astype(BF16)


def _inproj(u, w_in, rope, conv_w, conv_b, ln_w, ln_b, *, tm):
    b, l, d = u.shape
    hb = BF16_ROWS
    gpt = tm // GRID_W
    rcos, rsin, ccos, csin = rope
    tok = lambda i, t: (i, t, 0)
    prev = lambda i, t: (i, jnp.maximum(t * (tm // hb) - 1, 0), 0)
    nxt = lambda i, t: (i, jnp.minimum((t + 1) * (tm // hb), l // hb - 1), 0)
    rtab = lambda i, t: (t, 0, 0)
    vec = lambda a: a.reshape(1, d)
    widths = (RET_QK_W, RET_QK_W, RET_V_W, RET_V_W, D_MODEL, D_MODEL, D_MODEL)
    return pl.pallas_call(
        _inproj_kernel,
        grid=(b, l // tm),
        in_specs=[pl.BlockSpec((1, tm, d), tok),
                  pl.BlockSpec((1, hb, d), prev),
                  pl.BlockSpec((1, hb, d), nxt),
                  pl.BlockSpec(memory_space=pl.ANY),
                  pl.BlockSpec((1, gpt, LANES), rtab),
                  pl.BlockSpec((1, gpt, LANES), rtab),
                  _resident(ccos.shape),
                  _resident(csin.shape),
                  _resident(conv_w.shape),
                  _resident((1, d)), _resident((1, d)), _resident((1, d))],
        out_specs=[pl.BlockSpec((1, tm, w), tok) for w in widths],
        out_shape=[jax.ShapeDtypeStruct((b, l, w), BF16) for w in widths],
        scratch_shapes=[pltpu.VMEM((tm + HALO, d), BF16),
                        pltpu.VMEM((tm + HALO, d), F32),
                        pltpu.VMEM((tm, d), F32),
                        pltpu.VMEM(w_in.shape, BF16),
                        pltpu.VMEM((CAST_SLOTS, d // (2 * CAST_CHUNKS), IN_WIDTH), F32),
                        pltpu.SemaphoreType.DMA((CAST_SLOTS,))],
        compiler_params=_params(("arbitrary", "arbitrary")),
        name="inproj",
    )(u, u, u, w_in, rcos.reshape(l // tm, gpt, LANES), rsin.reshape(l // tm, gpt, LANES),
      ccos, csin, conv_w, vec(conv_b), vec(ln_w), vec(ln_b))


def _rope_tables(l):
    half = RET_QK_DIM // 2
    inv = ROPE_BASE ** (-jnp.arange(0, half, 2, dtype=F32) / half)

    def tabs(pos):
        ang = pos[:, None] * inv[None, :]
        return (jnp.concatenate([jnp.cos(ang), jnp.cos(ang)], axis=-1),
                jnp.concatenate([-jnp.sin(ang), jnp.sin(ang)], axis=-1))

    rcos, rsin = tabs(jnp.arange(l // GRID_W, dtype=F32))
    ccos, csin = tabs(jnp.arange(GRID_W, dtype=F32))
    return rcos, rsin, ccos, csin


def _ret_kernel(*refs, reverse):
    if reverse:
        lg_ref, lgf_ref, q_ref, k_ref, v_ref, s0_ref, o_ref, state, ubuf, ibuf = refs
    else:
        lg_ref, q_ref, k_ref, v_ref, s0_ref, part_ref, gnw_ref, o_ref, state, ubuf = refs
    h = pl.program_id(1)

    @pl.when(pl.program_id(2) == 0)
    def _():
        state[...] = s0_ref[0, 0]

    c = RET_SUB
    lg = lg_ref[h]
    pos = lax.broadcasted_iota(jnp.int32, (c, 1), 0).astype(F32)
    if reverse:
        diff = (lax.broadcasted_iota(jnp.int32, (c, c), 0)
                - lax.broadcasted_iota(jnp.int32, (c, c), 1)).astype(F32)
        dmat = jnp.where(diff < 0, jnp.exp(lg * jnp.maximum(-diff, 0.0)),
                         jnp.exp(lgf_ref[h] * jnp.maximum(diff, 0.0)))
        qdec = jnp.exp(lg * (c - pos))
        kdec = jnp.exp(lg * pos)
    else:
        qdec = jnp.exp(lg * (pos + 1.0))
        kdec = jnp.exp(lg * ((c - 1.0) - pos))
    cdec = jnp.exp(lg * jnp.full((1, 1), float(c), F32))

    nchunk = q_ref.shape[1] // c
    order = list(reversed(range(nchunk))) if reverse else list(range(nchunk))

    def local_terms(ci):
        rows = slice(ci * c, (ci + 1) * c)
        kc = k_ref[0, rows, :]
        vc = v_ref[0, rows, :]
        if reverse:
            s = lax.dot_general(q_ref[0, rows, :], kc, (((1,), (1,)), ((), ())), preferred_element_type=F32)
            ibuf[rows, :] = jnp.dot((s * dmat).astype(BF16), vc, preferred_element_type=F32)
        kd = (kc.astype(F32) * kdec).astype(BF16)
        ubuf[ci] = lax.dot_general(kd, vc, (((0,), (0,)), ((), ())), preferred_element_type=F32)

    local_terms(order[0])
    st = state[...]
    for n, ci in enumerate(order):
        if n + 1 < nchunk:
            local_terms(order[n + 1])
        rows = slice(ci * c, (ci + 1) * c)
        cross = jnp.dot(q_ref[0, rows, :], st.astype(BF16), preferred_element_type=F32) * qdec
        st = st * cdec + ubuf[ci]
        if ci == order[-1]:
            state[...] = st
        if reverse:
            o_ref[0, rows, :] = (ibuf[rows, :] + cross).astype(BF16)
        else:
            o = part_ref[0, rows, :].astype(F32) + cross
            mu = jnp.mean(o, axis=-1, keepdims=True)
            dlt = o - mu
            var = jnp.mean(dlt * dlt, axis=-1, keepdims=True)
            o_ref[0, rows, :] = ((dlt * lax.rsqrt(var + EPS)) * gnw_ref[...]).astype(BF16)


def _retention(lgs, q, k, v, s0, extra, *, reverse, t):
    b, l, _ = q.shape
    nt = l // t
    tmap = (lambda i, h, j: (i, nt - 1 - j, h)) if reverse else (lambda i, h, j: (i, j, h))
    in_specs = [pl.BlockSpec(memory_space=pltpu.SMEM)] * len(lgs)
    in_specs += [pl.BlockSpec((1, t, RET_QK_DIM), tmap),
                 pl.BlockSpec((1, t, RET_QK_DIM), tmap),
                 pl.BlockSpec((1, t, RET_V_DIM), tmap),
                 pl.BlockSpec((1, 1, RET_QK_DIM, RET_V_DIM), lambda i, h, j: (i, h, 0, 0))]
    scratch = [pltpu.VMEM((RET_QK_DIM, RET_V_DIM), F32),
               pltpu.VMEM((t // RET_SUB, RET_QK_DIM, RET_V_DIM), F32)]
    if reverse:
        scratch.append(pltpu.VMEM((t, RET_V_DIM), F32))
    else:
        in_specs += [pl.BlockSpec((1, t, RET_V_DIM), tmap),
                     pl.BlockSpec((1, RET_V_DIM), lambda i, h, j: (0, h))]
    return pl.pallas_call(
        functools.partial(_ret_kernel, reverse=reverse),
        grid=(b, RET_HEADS, nt),
        in_specs=in_specs,
        out_specs=pl.BlockSpec((1, t, RET_V_DIM), tmap),
        out_shape=jax.ShapeDtypeStruct((b, l, RET_V_W), BF16),
        scratch_shapes=scratch,
        compiler_params=_params(("parallel", "parallel", "arbitrary")),
        name="ret_bwd" if reverse else "ret_fwd",
    )(*lgs, q, k, v, s0, *extra)


def _mixer_kernel(on_ref, rg_ref, ys_ref, ga_ref, gb_ref, h_ref, mods_ref, wro_hbm, wco_hbm, wo_hbm, o_ref,
                  wro_ref, wco_ref, wo_ref, stage, sems):
    @pl.when(_first_step())
    def _():
        _load_cast_bf16(wro_hbm, wro_ref, stage, sems)
        _load_cast_bf16(wco_hbm, wco_ref, stage, sems)
        _load_cast_bf16(wo_hbm, wo_ref, stage, sems)

    y_conv = jnp.dot(ys_ref[0], wco_ref[...], preferred_element_type=F32)
    gated = (rg_ref[0].astype(F32) * on_ref[0].astype(F32)).astype(BF16)
    y_ret = jnp.dot(gated, wro_ref[...], preferred_element_type=F32)
    merged = (ga_ref[0].astype(F32) * y_ret + gb_ref[0].astype(F32) * y_conv).astype(BF16)
    m = jnp.dot(merged, wo_ref[...], preferred_element_type=F32)
    g2 = mods_ref[0, 5:6, :]
    o_ref[0] = h_ref[0] + g2 * m


def _mixer(on, srg, ys, sga, sgb, h, mods, wro, wco, wo, *, tm):
    b, l, d = h.shape
    tok = lambda i, t: (i, t, 0)
    hbm = pl.BlockSpec(memory_space=pl.ANY)
    return pl.pallas_call(
        _mixer_kernel,
        grid=(b, l // tm),
        in_specs=[pl.BlockSpec((1, tm, RET_V_W), tok),
                  pl.BlockSpec((1, tm, RET_V_W), tok),
                  pl.BlockSpec((1, tm, d), tok),
                  pl.BlockSpec((1, tm, d), tok),
                  pl.BlockSpec((1, tm, d), tok),
                  pl.BlockSpec((1, tm, d), tok),
                  pl.BlockSpec((1, N_MOD, d), lambda i, t: (i, 0, 0)),
                  hbm, hbm, hbm],
        out_specs=pl.BlockSpec((1, tm, d), tok),
        out_shape=jax.ShapeDtypeStruct((b, l, d), F32),
        scratch_shapes=[pltpu.VMEM(wro.shape, BF16),
                        pltpu.VMEM(wco.shape, BF16),
                        pltpu.VMEM(wo.shape, BF16),
                        pltpu.VMEM((CAST_SLOTS, d // (CAST_CHUNKS // 2), d), F32),
                        pltpu.SemaphoreType.DMA((CAST_SLOTS,))],
        compiler_params=_params(("arbitrary", "arbitrary")),
        name="mixer",
    )(on, srg, ys, sga, sgb, h, mods, wro, wco, wo)


def kernel(x, c, ctx, c_ctx, w_mod, b_mod, norm_ffn1, w_ffn1_in, w_ffn1_out, norm_mix, w_in, ret_decay_f, ret_decay_b, ret_gn_w, w_ret_out, conv_w, conv_b, conv_ln_w, conv_ln_b, w_conv_out, w_out, norm_ffn2, w_ffn2_in, w_ffn2_out, final_norm):
    assert w_mod.shape[0] == 1, "single-layer block"
    b, l, d = x.shape
    assert l % max(FFN_TM, RET_T, MIX_TM) == 0 and d == D_MODEL

    cc = jnp.zeros((SUBLANES, d), F32).at[:b].set(c).at[b].set(c_ctx)
    mods_all = _mods(cc, w_mod[0], b_mod[0])
    mods = mods_all[:b].reshape(b, N_MOD, d)
    mods_c = mods_all[b:b + 1].reshape(1, N_MOD, d)

    lgf = jax.nn.log_sigmoid(ret_decay_f[0].astype(F32))
    lgb = jax.nn.log_sigmoid(ret_decay_b[0].astype(F32))

    h, u = _ffn(x, mods, norm_ffn1[0], w_ffn1_in[0], w_ffn1_out[0], norm_mix[0], row0=0, final=False, tm=FFN_TM)
    n_ctx = ctx.shape[1]
    _, uc = _ffn(ctx.reshape(1, b * n_ctx, d), mods_c, norm_ffn1[0], w_ffn1_in[0], w_ffn1_out[0], norm_mix[0],
                 row0=0, final=False, tm=FFN_TM)

    rf, rb = _ctx_state(uc.reshape(b, n_ctx, d), w_in[0], lgf, lgb)

    q, k, v, srg, ys, sga, sgb = _inproj(u, w_in[0], _rope_tables(l), conv_w[0], conv_b[0],
                                         conv_ln_w[0], conv_ln_b[0], tm=INPROJ_TM)

    part = _retention((lgb, lgf), q, k, v, rb, (), reverse=True, t=RET_T)
    on = _retention((lgf,), q, k, v, rf, (part, ret_gn_w[0].reshape(1, RET_V_W)), reverse=False, t=RET_T)

    h2 = _mixer(on, srg, ys, sga, sgb, h, mods, w_ret_out[0], w_conv_out[0], w_out[0], tm=MIX_TM)

    return _ffn(h2, mods, norm_ffn2[0], w_ffn2_in[0], w_ffn2_out[0], final_norm, row0=6, final=True, tm=FFN_TM)
```

```python
import functools

import jax
import jax.numpy as jnp
from jax import lax
from jax.experimental import pallas as pl
from jax.experimental.pallas import tpu as pltpu

F32 = jnp.float32
BF16 = jnp.bfloat16

D_MODEL = 1024
GRID_W = 64
RET_HEADS = 4
RET_QK_DIM = 256
RET_V_DIM = 512
CONV_WIDTH = 31
CONV_PAD = CONV_WIDTH // 2
D_FF = 2816
ROPE_BASE = 10000.0
EPS = 1e-6
N_MOD = 9

RET_QK_W = RET_HEADS * RET_QK_DIM
RET_V_W = RET_HEADS * RET_V_DIM
K0 = RET_QK_W
V0 = 2 * RET_QK_W
G0 = V0 + RET_V_W
C0 = G0 + RET_V_W
GA0 = C0 + 2 * D_MODEL
GB0 = GA0 + D_MODEL
IN_WIDTH = GB0 + D_MODEL

V7X_VMEM_LIMIT_BYTES = 56 * 1024 * 1024
SUBLANES = 8
LANES = 128
BF16_ROWS = 16
HALO = 2 * BF16_ROWS

MODS_TN = 2304
CAST_CHUNKS = 16
CAST_SLOTS = 4
FFN_CHUNK = 256
FFN_TM = 1024
INPROJ_TM = 512
INPROJ_TN = 256
MIX_TM = 512
RET_T = 4096
RET_SUB = 256
CONV_RB = 64


def _resident(shape):
    nd = len(shape)
    return pl.BlockSpec(shape, lambda *_: (0,) * nd, pipeline_mode=pl.Buffered(1))


def _params(sem):
    return pltpu.CompilerParams(dimension_semantics=sem, vmem_limit_bytes=V7X_VMEM_LIMIT_BYTES)


def _first_step():
    return (pl.program_id(0) == 0) & (pl.program_id(1) == 0)


def _load_cast_bf16(src_hbm, dst, stage, sems, scaled_cols=None):
    nslot, rows = stage.shape[0], stage.shape[1]
    n = src_hbm.shape[0] // rows
    assert n * rows == src_hbm.shape[0] and n >= nslot

    def copy(i):
        return pltpu.make_async_copy(src_hbm.at[pl.ds(i * rows, rows)], stage.at[i % nslot], sems.at[i % nslot])

    for i in range(nslot - 1):
        copy(i).start()
    for i in range(n):
        if i + nslot - 1 < n:
            copy(i + nslot - 1).start()
        copy(i).wait()
        rs = slice(i * rows, (i + 1) * rows)
        if scaled_cols is None:
            dst[rs, :] = stage[i % nslot].astype(BF16)
        else:
            c0, c1, s = scaled_cols
            dst[rs, :c0] = stage[i % nslot, :, :c0].astype(BF16)
            dst[rs, c0:c1] = (stage[i % nslot, :, c0:c1] * s).astype(BF16)
            dst[rs, c1:] = stage[i % nslot, :, c1:].astype(BF16)


def _rms_mod(x, w, scale, shift):
    y = x * lax.rsqrt(jnp.mean(x * x, axis=-1, keepdims=True) + EPS)
    return (y * w) * (1.0 + scale) + shift


def _layernorm(x, w, b):
    mu = jnp.mean(x, axis=-1, keepdims=True)
    d = x - mu
    var = jnp.mean(d * d, axis=-1, keepdims=True)
    return (d * lax.rsqrt(var + EPS)) * w + b


def _split_bf16(a):
    hi = a.astype(BF16)
    return hi, (a - hi.astype(F32)).astype(BF16)


def _mods_kernel(c_ref, w_ref, b_ref, o_ref):
    c = c_ref[...]
    s_hi, s_lo = _split_bf16(c * jax.nn.sigmoid(c))
    w_hi, w_lo = _split_bf16(w_ref[...])
    dot = functools.partial(jnp.dot, preferred_element_type=F32)
    o_ref[...] = dot(s_hi, w_hi) + (dot(s_hi, w_lo) + dot(s_lo, w_hi)) + b_ref[...]


def _mods(cc, w_mod, b_mod):
    rows, d = cc.shape
    n = w_mod.shape[1]
    tn = MODS_TN
    return pl.pallas_call(
        _mods_kernel,
        grid=(n // tn,),
        in_specs=[pl.BlockSpec((rows, d), lambda j: (0, 0)),
                  pl.BlockSpec((d, tn), lambda j: (0, j)),
                  pl.BlockSpec((1, tn), lambda j: (0, j))],
        out_specs=pl.BlockSpec((rows, tn), lambda j: (0, j)),
        out_shape=jax.ShapeDtypeStruct((rows, n), F32),
        compiler_params=_params(("arbitrary",)),
        name="mods",
    )(cc, w_mod, b_mod.reshape(1, n))


def _ffn_kernel(x_ref, mods_ref, nw_ref, w1_hbm, w2_hbm, n2w_ref, *rest, row0, final):
    n_out = 1 if final else 2
    out_refs = rest[:n_out]
    w1_ref, w2_ref, stage1, stage2, sems = rest[n_out:]

    @pl.when(_first_step())
    def _():
        _load_cast_bf16(w1_hbm, w1_ref, stage1, sems.at[0])
        _load_cast_bf16(w2_hbm, w2_ref, stage2, sems.at[1])

    x = x_ref[0]
    sh = mods_ref[0, row0:row0 + 1, :]
    sc = mods_ref[0, row0 + 1:row0 + 2, :]
    g = mods_ref[0, row0 + 2:row0 + 3, :]
    xm = _rms_mod(x, nw_ref[...], sc, sh).astype(BF16)
    ck = FFN_CHUNK
    acc = jnp.zeros_like(x)
    for c0 in range(0, D_FF, ck):
        a = jnp.dot(xm, w1_ref[:, c0:c0 + ck], preferred_element_type=F32)
        b = jnp.dot(xm, w1_ref[:, D_FF + c0:D_FF + c0 + ck], preferred_element_type=F32)
        act = (a * jax.nn.sigmoid(a) * b).astype(BF16)
        acc = acc + jnp.dot(act, w2_ref[c0:c0 + ck, :], preferred_element_type=F32)
    h = x + (0.5 * g) * acc
    if final:
        (o_ref,) = out_refs
        y = h * lax.rsqrt(jnp.mean(h * h, axis=-1, keepdims=True) + EPS)
        o_ref[0] = y * n2w_ref[...]
    else:
        h_ref, u_ref = out_refs
        h_ref[0] = h
        sh2 = mods_ref[0, row0 + 3:row0 + 4, :]
        sc2 = mods_ref[0, row0 + 4:row0 + 5, :]
        u_ref[0] = _rms_mod(h, n2w_ref[...], sc2, sh2).astype(BF16)


def _ffn(x, mods, norm_w, w1, w2, norm2_w, *, row0, final, tm):
    b, l, d = x.shape
    tm = min(tm, l)
    mods_map = (lambda i, t: (i, 0, 0)) if mods.shape[0] == b else (lambda i, t: (0, 0, 0))
    tok = lambda i, t: (i, t, 0)
    hbm = pl.BlockSpec(memory_space=pl.ANY)
    in_specs = [pl.BlockSpec((1, tm, d), tok),
                pl.BlockSpec((1, N_MOD, d), mods_map),
                _resident((1, d)),
                hbm,
                hbm,
                _resident((1, d))]
    scratch = [pltpu.VMEM(w1.shape, BF16),
               pltpu.VMEM(w2.shape, BF16),
               pltpu.VMEM((CAST_SLOTS, w1.shape[0] // CAST_CHUNKS, w1.shape[1]), F32),
               pltpu.VMEM((CAST_SLOTS, w2.shape[0] // CAST_CHUNKS, w2.shape[1]), F32),
               pltpu.SemaphoreType.DMA((2, CAST_SLOTS))]
    if final:
        out_specs = pl.BlockSpec((1, tm, d), tok)
        out_shape = jax.ShapeDtypeStruct((b, l, d), F32)
    else:
        out_specs = [pl.BlockSpec((1, tm, d), tok), pl.BlockSpec((1, tm, d), tok)]
        out_shape = [jax.ShapeDtypeStruct((b, l, d), F32), jax.ShapeDtypeStruct((b, l, d), BF16)]
    return pl.pallas_call(
        functools.partial(_ffn_kernel, row0=row0, final=final),
        grid=(b, l // tm),
        in_specs=in_specs,
        out_specs=out_specs,
        out_shape=out_shape,
        scratch_shapes=scratch,
        compiler_params=_params(("arbitrary", "arbitrary")),
        name="ffn_final" if final else "ffn_mix",
    )(x, mods, norm_w.reshape(1, d), w1, w2, norm2_w.reshape(1, d))


def _ctx_state_kernel(lgf_ref, lgb_ref, u_ref, wk_ref, wv_ref, rf_ref, rb_ref):
    h = pl.program_id(0)
    u = u_ref[0]
    n = u.shape[0]
    k = jnp.dot(u, wk_ref[...].astype(BF16), preferred_element_type=F32) * (RET_QK_DIM ** -0.5)
    v = jnp.dot(u, wv_ref[...].astype(BF16), preferred_element_type=F32).astype(BF16)
    pos = lax.broadcasted_iota(jnp.int32, (n, 1), 0).astype(F32)
    wf = jnp.exp(lgf_ref[h] * ((n - 1.0) - pos))
    wb = jnp.exp(lgb_ref[h] * pos)
    tdot = lambda a, c: lax.dot_general(a, c, (((0,), (0,)), ((), ())), preferred_element_type=F32)
    rf_ref[0, 0] = tdot((k * wf).astype(BF16), v)
    rb_ref[0, 0] = tdot((k * wb).astype(BF16), v)


def _ctx_state(uc, w_in, lgf, lgb):
    b, n, d = uc.shape
    smem = pl.BlockSpec(memory_space=pltpu.SMEM)
    st = jax.ShapeDtypeStruct((b, RET_HEADS, RET_QK_DIM, RET_V_DIM), F32)
    st_spec = pl.BlockSpec((1, 1, RET_QK_DIM, RET_V_DIM), lambda h, i: (i, h, 0, 0))
    return pl.pallas_call(
        _ctx_state_kernel,
        grid=(RET_HEADS, b),
        in_specs=[smem, smem,
                  pl.BlockSpec((1, n, d), lambda h, i: (i, 0, 0)),
                  pl.BlockSpec((d, RET_QK_DIM), lambda h, i: (0, K0 // RET_QK_DIM + h)),
                  pl.BlockSpec((d, RET_V_DIM), lambda h, i: (0, V0 // RET_V_DIM + h))],
        out_specs=[st_spec, st_spec],
        out_shape=[st, st],
        compiler_params=_params(("parallel", "parallel")),
        name="ctx_state",
    )(lgf, lgb, uc, w_in, w_in)


def _depthwise_conv(ybuf, cw_ref, cbuf, c0, r0, zero):
    rows = CONV_RB + HALO
    base = BF16_ROWS - CONV_PAD
    cs = slice(c0, c0 + LANES)
    src = ybuf[r0:r0 + rows, cs]
    parts = [None] * (HALO // SUBLANES)
    for r in range(SUBLANES):
        zr = src if r == 0 else pltpu.roll(src, rows - r, 0)
        for ai, a0 in enumerate(range(0, HALO, SUBLANES)):
            j = a0 + r - base
            if 0 <= j < CONV_WIDTH:
                term = zr[a0:a0 + CONV_RB, :] * cw_ref[j:j + 1, cs]
                parts[ai] = term if parts[ai] is None else parts[ai] + term
    acc = (parts[0] + parts[1]) + (parts[2] + parts[3])
    cbuf[r0:r0 + CONV_RB, cs] = acc + jnp.tile(zero, (CONV_RB // SUBLANES, 1))


def _inproj_kernel(u_ref, uprev_ref, unext_ref, w_hbm, rcos_ref, rsin_ref, ccos_ref, csin_ref,
                   cw_ref, cb_ref, lnw_ref, lnb_ref,
                   q_ref, k_ref, v_ref, rg_ref, ys_ref, ga_ref, gb_ref,
                   uext, ybuf, cbuf, w_ref, stage, sems):
    @pl.when(_first_step())
    def _():
        _load_cast_bf16(w_hbm, w_ref, stage, sems, scaled_cols=(K0, V0, RET_QK_DIM ** -0.5))

    t = pl.program_id(1)
    tm = u_ref.shape[1]
    hb = BF16_ROWS
    u = u_ref[0]

    def proj(c0):
        return jnp.dot(u, w_ref[:, c0:c0 + INPROJ_TN], preferred_element_type=F32)

    def rope(val, ref, o0):
        for gi in range(tm // GRID_W):
            rs = slice(gi * GRID_W, (gi + 1) * GRID_W)
            for l0 in range(0, INPROJ_TN, LANES):
                if (o0 + l0) % RET_QK_DIM == 0:
                    cs, sn = rcos_ref[0, gi:gi + 1, :], rsin_ref[0, gi:gi + 1, :]
                else:
                    cs, sn = ccos_ref[...], csin_ref[...]
                slab = val[rs, l0:l0 + LANES]
                ref[0, rs, o0 + l0:o0 + l0 + LANES] = (
                    slab * cs + pltpu.roll(slab, LANES // 2, 1) * sn).astype(BF16)

    def piece(c0):
        val = proj(c0)
        if c0 < K0:
            rope(val, q_ref, c0)
        elif c0 < V0:
            rope(val, k_ref, c0 - K0)
        elif c0 < G0:
            v_ref[0, :, c0 - V0:c0 - V0 + INPROJ_TN] = val.astype(BF16)
        elif c0 < C0:
            rg_ref[0, :, c0 - G0:c0 - G0 + INPROJ_TN] = (val * jax.nn.sigmoid(val)).astype(BF16)
        elif c0 < GB0:
            ga_ref[0, :, c0 - GA0:c0 - GA0 + INPROJ_TN] = jax.nn.sigmoid(val).astype(BF16)
        else:
            gb_ref[0, :, c0 - GB0:c0 - GB0 + INPROJ_TN] = jax.nn.sigmoid(val).astype(BF16)
        return val

    uext[0:hb, :] = jnp.where(t > 0, uprev_ref[0], jnp.zeros_like(uprev_ref[0]))
    uext[hb:hb + tm, :] = u
    uext[hb + tm:, :] = jnp.where(t < pl.num_programs(1) - 1, unext_ref[0], jnp.zeros_like(unext_ref[0]))
    ue = uext[...]
    ca = jnp.dot(ue, w_ref[:, C0:C0 + D_MODEL], preferred_element_type=F32)
    cg = jnp.dot(ue, w_ref[:, C0 + D_MODEL:GA0], preferred_element_type=F32)
    ybuf[...] = ca * jax.nn.sigmoid(cg)

    cols = [c0 for c0 in range(0, IN_WIDTH, INPROJ_TN) if not C0 <= c0 < GA0]
    blocks = [(c0, r0) for c0 in range(0, D_MODEL, LANES) for r0 in range(0, tm, CONV_RB)]
    per_tile = len(blocks) // len(cols)
    assert per_tile * len(cols) == len(blocks)
    zero = jnp.zeros((SUBLANES, LANES), F32)
    for i, c0 in enumerate(cols):
        for b0, r0 in blocks[i * per_tile:(i + 1) * per_tile]:
            _depthwise_conv(ybuf, cw_ref, cbuf, b0, r0, zero)
        bits = piece(c0)[0:SUBLANES, 0:LANES].astype(jnp.int32)
        zero = lax.shift_right_logical(lax.shift_right_logical(bits, 16), 16).astype(F32)

    yn = _layernorm(cbuf[...] + cb_ref[...], lnw_ref[...], lnb_ref[...])
    ys_ref[0] = (yn * jax.nn.sigmoid(yn)).astype(BF16)


def _inproj(u, w_in, rope, conv_w, conv_b, ln_w, ln_b, *, tm):
    b, l, d = u.shape
    hb = BF16_ROWS
    gpt = tm // GRID_W
    rcos, rsin, ccos, csin = rope
    tok = lambda i, t: (i, t, 0)
    prev = lambda i, t: (i, jnp.maximum(t * (tm // hb) - 1, 0), 0)
    nxt = lambda i, t: (i, jnp.minimum((t + 1) * (tm // hb), l // hb - 1), 0)
    rtab = lambda i, t: (t, 0, 0)
    vec = lambda a: a.reshape(1, d)
    widths = (RET_QK_W, RET_QK_W, RET_V_W, RET_V_W, D_MODEL, D_MODEL, D_MODEL)
    return pl.pallas_call(
        _inproj_kernel,
        grid=(b, l // tm),
        in_specs=[pl.BlockSpec((1, tm, d), tok),
                  pl.BlockSpec((1, hb, d), prev),
                  pl.BlockSpec((1, hb, d), nxt),
                  pl.BlockSpec(memory_space=pl.ANY),
                  pl.BlockSpec((1, gpt, LANES), rtab),
                  pl.BlockSpec((1, gpt, LANES), rtab),
                  _resident(ccos.shape),
                  _resident(csin.shape),
                  _resident(conv_w.shape),
                  _resident((1, d)), _resident((1, d)), _resident((1, d))],
        out_specs=[pl.BlockSpec((1, tm, w), tok) for w in widths],
        out_shape=[jax.ShapeDtypeStruct((b, l, w), BF16) for w in widths],
        scratch_shapes=[pltpu.VMEM((tm + HALO, d), BF16),
                        pltpu.VMEM((tm + HALO, d), F32),
                        pltpu.VMEM((tm, d), F32),
                        pltpu.VMEM(w_in.shape, BF16),
                        pltpu.VMEM((CAST_SLOTS, d // (2 * CAST_CHUNKS), IN_WIDTH), F32),
                        pltpu.SemaphoreType.DMA((CAST_SLOTS,))],
        compiler_params=_params(("arbitrary", "arbitrary")),
        name="inproj",
    )(u, u, u, w_in, rcos.reshape(l // tm, gpt, LANES), rsin.reshape(l // tm, gpt, LANES),
      ccos, csin, conv_w, vec(conv_b), vec(ln_w), vec(ln_b))


def _rope_tables(l):
    half = RET_QK_DIM // 2
    inv = ROPE_BASE ** (-jnp.arange(0, half, 2, dtype=F32) / half)

    def tabs(pos):
        ang = pos[:, None] * inv[None, :]
        return (jnp.concatenate([jnp.cos(ang), jnp.cos(ang)], axis=-1),
                jnp.concatenate([-jnp.sin(ang), jnp.sin(ang)], axis=-1))

    rcos, rsin = tabs(jnp.arange(l // GRID_W, dtype=F32))
    ccos, csin = tabs(jnp.arange(GRID_W, dtype=F32))
    return rcos, rsin, ccos, csin


def _ret_kernel(*refs, reverse):
    if reverse:
        lg_ref, lgf_ref, q_ref, k_ref, v_ref, s0_ref, o_ref, state, ubuf, ibuf = refs
    else:
        lg_ref, q_ref, k_ref, v_ref, s0_ref, part_ref, gnw_ref, o_ref, state, ubuf = refs
    h = pl.program_id(1)

    @pl.when(pl.program_id(2) == 0)
    def _():
        state[...] = s0_ref[0, 0]

    c = RET_SUB
    lg = lg_ref[h]
    pos = lax.broadcasted_iota(jnp.int32, (c, 1), 0).astype(F32)
    if reverse:
        diff = (lax.broadcasted_iota(jnp.int32, (c, c), 0)
                - lax.broadcasted_iota(jnp.int32, (c, c), 1)).astype(F32)
        dmat = jnp.where(diff < 0, jnp.exp(lg * jnp.maximum(-diff, 0.0)),
                         jnp.exp(lgf_ref[h] * jnp.maximum(diff, 0.0)))
        qdec = jnp.exp(lg * (c - pos))
        kdec = jnp.exp(lg * pos)
    else:
        qdec = jnp.exp(lg * (pos + 1.0))
        kdec = jnp.exp(lg * ((c - 1.0) - pos))
    cdec = jnp.exp(lg * jnp.full((1, 1), float(c), F32))

    nchunk = q_ref.shape[1] // c
    order = list(reversed(range(nchunk))) if reverse else list(range(nchunk))

    def local_terms(ci):
        rows = slice(ci * c, (ci + 1) * c)
        kc = k_ref[0, rows, :]
        vc = v_ref[0, rows, :]
        if reverse:
            s = lax.dot_general(q_ref[0, rows, :], kc, (((1,), (1,)), ((), ())), preferred_element_type=F32)
            ibuf[rows, :] = jnp.dot((s * dmat).astype(BF16), vc, preferred_element_type=F32)
        kd = (kc.astype(F32) * kdec).astype(BF16)
        ubuf[ci] = lax.dot_general(kd, vc, (((0,), (0,)), ((), ())), preferred_element_type=F32)

    local_terms(order[0])
    st = state[...]
    for n, ci in enumerate(order):
        if n + 1 < nchunk:
            local_terms(order[n + 1])
        rows = slice(ci * c, (ci + 1) * c)
        cross = jnp.dot(q_ref[0, rows, :], st.astype(BF16), preferred_element_type=F32) * qdec
        st = st * cdec + ubuf[ci]
        if ci == order[-1]:
            state[...] = st
        if reverse:
            o_ref[0, rows, :] = (ibuf[rows, :] + cross).astype(BF16)
        else:
            o = part_ref[0, rows, :].astype(F32) + cross
            mu = jnp.mean(o, axis=-1, keepdims=True)
            dlt = o - mu
            var = jnp.mean(dlt * dlt, axis=-1, keepdims=True)
            o_ref[0, rows, :] = ((dlt * lax.rsqrt(var + EPS)) * gnw_ref[...]).astype(BF16)


def _retention(lgs, q, k, v, s0, extra, *, reverse, t):
    b, l, _ = q.shape
    nt = l // t
    tmap = (lambda i, h, j: (i, nt - 1 - j, h)) if reverse else (lambda i, h, j: (i, j, h))
    in_specs = [pl.BlockSpec(memory_space=pltpu.SMEM)] * len(lgs)
    in_specs += [pl.BlockSpec((1, t, RET_QK_DIM), tmap),
                 pl.BlockSpec((1, t, RET_QK_DIM), tmap),
                 pl.BlockSpec((1, t, RET_V_DIM), tmap),
                 pl.BlockSpec((1, 1, RET_QK_DIM, RET_V_DIM), lambda i, h, j: (i, h, 0, 0))]
    scratch = [pltpu.VMEM((RET_QK_DIM, RET_V_DIM), F32),
               pltpu.VMEM((t // RET_SUB, RET_QK_DIM, RET_V_DIM), F32)]
    if reverse:
        scratch.append(pltpu.VMEM((t, RET_V_DIM), F32))
    else:
        in_specs += [pl.BlockSpec((1, t, RET_V_DIM), tmap),
                     pl.BlockSpec((1, RET_V_DIM), lambda i, h, j: (0, h))]
    return pl.pallas_call(
        functools.partial(_ret_kernel, reverse=reverse),
        grid=(b, RET_HEADS, nt),
        in_specs=in_specs,
        out_specs=pl.BlockSpec((1, t, RET_V_DIM), tmap),
        out_shape=jax.ShapeDtypeStruct((b, l, RET_V_W), BF16),
        scratch_shapes=scratch,
        compiler_params=_params(("parallel", "parallel", "arbitrary")),
        name="ret_bwd" if reverse else "ret_fwd",
    )(*lgs, q, k, v, s0, *extra)


def _mixer_kernel(on_ref, rg_ref, ys_ref, ga_ref, gb_ref, h_ref, mods_ref, wro_hbm, wco_hbm, wo_hbm, o_ref,
                  wro_ref, wco_ref, wo_ref, stage, sems):
    @pl.when(_first_step())
    def _():
        _load_cast_bf16(wro_hbm, wro_ref, stage, sems)
        _load_cast_bf16(wco_hbm, wco_ref, stage, sems)
        _load_cast_bf16(wo_hbm, wo_ref, stage, sems)

    y_conv = jnp.dot(ys_ref[0], wco_ref[...], preferred_element_type=F32)
    gated = (rg_ref[0].astype(F32) * on_ref[0].astype(F32)).astype(BF16)
    y_ret = jnp.dot(gated, wro_ref[...], preferred_element_type=F32)
    merged = (ga_ref[0].astype(F32) * y_ret + gb_ref[0].astype(F32) * y_conv).astype(BF16)
    m = jnp.dot(merged, wo_ref[...], preferred_element_type=F32)
    g2 = mods_ref[0, 5:6, :]
    o_ref[0] = h_ref[0] + g2 * m


def _mixer(on, srg, ys, sga, sgb, h, mods, wro, wco, wo, *, tm):
    b, l, d = h.shape
    tok = lambda i, t: (i, t, 0)
    hbm = pl.BlockSpec(memory_space=pl.ANY)
    return pl.pallas_call(
        _mixer_kernel,
        grid=(b, l // tm),
        in_specs=[pl.BlockSpec((1, tm, RET_V_W), tok),
                  pl.BlockSpec((1, tm, RET_V_W), tok),
                  pl.BlockSpec((1, tm, d), tok),
                  pl.BlockSpec((1, tm, d), tok),
                  pl.BlockSpec((1, tm, d), tok),
                  pl.BlockSpec((1, tm, d), tok),
                  pl.BlockSpec((1, N_MOD, d), lambda i, t: (i, 0, 0)),
                  hbm, hbm, hbm],
        out_specs=pl.BlockSpec((1, tm, d), tok),
        out_shape=jax.ShapeDtypeStruct((b, l, d), F32),
        scratch_shapes=[pltpu.VMEM(wro.shape, BF16),
                        pltpu.VMEM(wco.shape, BF16),
                        pltpu.VMEM(wo.shape, BF16),
                        pltpu.VMEM((CAST_SLOTS, d // (CAST_CHUNKS // 2), d), F32),
                        pltpu.SemaphoreType.DMA((CAST_SLOTS,))],
        compiler_params=_params(("arbitrary", "arbitrary")),
        name="mixer",
    )(on, srg, ys, sga, sgb, h, mods, wro, wco, wo)


def kernel(x, c, ctx, c_ctx, w_mod, b_mod, norm_ffn1, w_ffn1_in, w_ffn1_out, norm_mix, w_in, ret_decay_f, ret_decay_b, ret_gn_w, w_ret_out, conv_w, conv_b, conv_ln_w, conv_ln_b, w_conv_out, w_out, norm_ffn2, w_ffn2_in, w_ffn2_out, final_norm):
    assert w_mod.shape[0] == 1, "single-layer block"
    b, l, d = x.shape
    assert l % max(FFN_TM, RET_T, MIX_TM) == 0 and d == D_MODEL

    cc = jnp.zeros((SUBLANES, d), F32).at[:b].set(c).at[b].set(c_ctx)
    mods_all = _mods(cc, w_mod[0], b_mod[0])
    mods = mods_all[:b].reshape(b, N_MOD, d)
    mods_c = mods_all[b:b + 1].reshape(1, N_MOD, d)

    lgf = jax.nn.log_sigmoid(ret_decay_f[0].astype(F32))
    lgb = jax.nn.log_sigmoid(ret_decay_b[0].astype(F32))

    h, u = _ffn(x, mods, norm_ffn1[0], w_ffn1_in[0], w_ffn1_out[0], norm_mix[0], row0=0, final=False, tm=FFN_TM)
    n_ctx = ctx.shape[1]
    _, uc = _ffn(ctx.reshape(1, b * n_ctx, d), mods_c, norm_ffn1[0], w_ffn1_in[0], w_ffn1_out[0], norm_mix[0],
                 row0=0, final=False, tm=FFN_TM)

    rf, rb = _ctx_state(uc.reshape(b, n_ctx, d), w_in[0], lgf, lgb)

    q, k, v, srg, ys, sga, sgb = _inproj(u, w_in[0], _rope_tables(l), conv_w[0], conv_b[0],
                                         conv_ln_w[0], conv_ln_b[0], tm=INPROJ_TM)

    part = _retention((lgb, lgf), q, k, v, rb, (), reverse=True, t=RET_T)
    on = _retention((lgf,), q, k, v, rf, (part, ret_gn_w[0].reshape(1, RET_V_W)), reverse=False, t=RET_T)

    h2 = _mixer(on, srg, ys, sga, sgb, h, mods, w_ret_out[0], w_conv_out[0], w_out[0], tm=MIX_TM)

    return _ffn(h2, mods, norm_ffn2[0], w_ffn2_in[0], w_ffn2_out[0], final_norm, row0=6, final=True, tm=FFN_TM)
```

```python
import functools

import jax
import jax.numpy as jnp
from jax import lax
from jax.experimental import pallas as pl
from jax.experimental.pallas import tpu as pltpu

F32 = jnp.float32
BF16 = jnp.bfloat16

D_MODEL = 1024
GRID_W = 64
RET_HEADS = 4
RET_QK_DIM = 256
RET_V_DIM = 512
CONV_WIDTH = 31
CONV_PAD = CONV_WIDTH // 2
D_FF = 2816
ROPE_BASE = 10000.0
EPS = 1e-6
N_MOD = 9

RET_QK_W = RET_HEADS * RET_QK_DIM
RET_V_W = RET_HEADS * RET_V_DIM
K0 = RET_QK_W
V0 = 2 * RET_QK_W
G0 = V0 + RET_V_W
C0 = G0 + RET_V_W
GA0 = C0 + 2 * D_MODEL
GB0 = GA0 + D_MODEL
IN_WIDTH = GB0 + D_MODEL

V7X_VMEM_LIMIT_BYTES = 56 * 1024 * 1024
SUBLANES = 8
LANES = 128
BF16_ROWS = 16
HALO = 2 * BF16_ROWS

MODS_TN = 2304
CAST_CHUNKS = 16
CAST_SLOTS = 4
FFN_CHUNK = 256
FFN_TM = 1024
INPROJ_TM = 512
INPROJ_TN = 256
MIX_TM = 512
RET_T = 4096
RET_SUB = 256
CONV_RB = 64


def _resident(shape):
    nd = len(shape)
    return pl.BlockSpec(shape, lambda *_: (0,) * nd, pipeline_mode=pl.Buffered(1))


def _params(sem):
    return pltpu.CompilerParams(dimension_semantics=sem, vmem_limit_bytes=V7X_VMEM_LIMIT_BYTES)


def _first_step():
    return (pl.program_id(0) == 0) & (pl.program_id(1) == 0)


def _load_cast_bf16(src_hbm, dst, stage, sems, scaled_cols=None):
    nslot, rows = stage.shape[0], stage.shape[1]
    n = src_hbm.shape[0] // rows
    assert n * rows == src_hbm.shape[0] and n >= nslot

    def copy(i):
        return pltpu.make_async_copy(src_hbm.at[pl.ds(i * rows, rows)], stage.at[i % nslot], sems.at[i % nslot])

    for i in range(nslot - 1):
        copy(i).start()
    for i in range(n):
        if i + nslot - 1 < n:
            copy(i + nslot - 1).start()
        copy(i).wait()
        rs = slice(i * rows, (i + 1) * rows)
        if scaled_cols is None:
            dst[rs, :] = stage[i % nslot].astype(BF16)
        else:
            c0, c1, s = scaled_cols
            dst[rs, :c0] = stage[i % nslot, :, :c0].astype(BF16)
            dst[rs, c0:c1] = (stage[i % nslot, :, c0:c1] * s).astype(BF16)
            dst[rs, c1:] = stage[i % nslot, :, c1:].astype(BF16)


def _rms_mod(x, w, scale, shift):
    y = x * lax.rsqrt(jnp.mean(x * x, axis=-1, keepdims=True) + EPS)
    return (y * w) * (1.0 + scale) + shift


def _layernorm(x, w, b):
    mu = jnp.mean(x, axis=-1, keepdims=True)
    d = x - mu
    var = jnp.mean(d * d, axis=-1, keepdims=True)
    return (d * lax.rsqrt(var + EPS)) * w + b


def _split_bf16(a):
    hi = a.astype(BF16)
    return hi, (a - hi.astype(F32)).astype(BF16)


def _mods_kernel(c_ref, w_ref, b_ref, o_ref):
    c = c_ref[...]
    s_hi, s_lo = _split_bf16(c * jax.nn.sigmoid(c))
    w_hi, w_lo = _split_bf16(w_ref[...])
    dot = functools.partial(jnp.dot, preferred_element_type=F32)
    o_ref[...] = dot(s_hi, w_hi) + (dot(s_hi, w_lo) + dot(s_lo, w_hi)) + b_ref[...]


def _mods(cc, w_mod, b_mod):
    rows, d = cc.shape
    n = w_mod.shape[1]
    tn = MODS_TN
    return pl.pallas_call(
        _mods_kernel,
        grid=(n // tn,),
        in_specs=[pl.BlockSpec((rows, d), lambda j: (0, 0)),
                  pl.BlockSpec((d, tn), lambda j: (0, j)),
                  pl.BlockSpec((1, tn), lambda j: (0, j))],
        out_specs=pl.BlockSpec((rows, tn), lambda j: (0, j)),
        out_shape=jax.ShapeDtypeStruct((rows, n), F32),
        compiler_params=_params(("arbitrary",)),
        name="mods",
    )(cc, w_mod, b_mod.reshape(1, n))


def _ffn_kernel(x_ref, mods_ref, nw_ref, w1_hbm, w2_hbm, n2w_ref, *rest, row0, final):
    n_out = 1 if final else 2
    out_refs = rest[:n_out]
    w1_ref, w2_ref, stage1, stage2, sems = rest[n_out:]

    @pl.when(_first_step())
    def _():
        _load_cast_bf16(w1_hbm, w1_ref, stage1, sems.at[0])
        _load_cast_bf16(w2_hbm, w2_ref, stage2, sems.at[1])

    x = x_ref[0]
    sh = mods_ref[0, row0:row0 + 1, :]
    sc = mods_ref[0, row0 + 1:row0 + 2, :]
    g = mods_ref[0, row0 + 2:row0 + 3, :]
    xm = _rms_mod(x, nw_ref[...], sc, sh).astype(BF16)
    ck = FFN_CHUNK
    acc = jnp.zeros_like(x)
    for c0 in range(0, D_FF, ck):
        a = jnp.dot(xm, w1_ref[:, c0:c0 + ck], preferred_element_type=F32)
        b = jnp.dot(xm, w1_ref[:, D_FF + c0:D_FF + c0 + ck], preferred_element_type=F32)
        act = (a * jax.nn.sigmoid(a) * b).astype(BF16)
        acc = acc + jnp.dot(act, w2_ref[c0:c0 + ck, :], preferred_element_type=F32)
    h = x + (0.5 * g) * acc
    if final:
        (o_ref,) = out_refs
        y = h * lax.rsqrt(jnp.mean(h * h, axis=-1, keepdims=True) + EPS)
        o_ref[0] = y * n2w_ref[...]
    else:
        h_ref, u_ref = out_refs
        h_ref[0] = h
        sh2 = mods_ref[0, row0 + 3:row0 + 4, :]
        sc2 = mods_ref[0, row0 + 4:row0 + 5, :]
        u_ref[0] = _rms_mod(h, n2w_ref[...], sc2, sh2).astype(BF16)


def _ffn(x, mods, norm_w, w1, w2, norm2_w, *, row0, final, tm):
    b, l, d = x.shape
    tm = min(tm, l)
    mods_map = (lambda i, t: (i, 0, 0)) if mods.shape[0] == b else (lambda i, t: (0, 0, 0))
    tok = lambda i, t: (i, t, 0)
    hbm = pl.BlockSpec(memory_space=pl.ANY)
    in_specs = [pl.BlockSpec((1, tm, d), tok),
                pl.BlockSpec((1, N_MOD, d), mods_map),
                _resident((1, d)),
                hbm,
                hbm,
                _resident((1, d))]
    scratch = [pltpu.VMEM(w1.shape, BF16),
               pltpu.VMEM(w2.shape, BF16),
               pltpu.VMEM((CAST_SLOTS, w1.shape[0] // CAST_CHUNKS, w1.shape[1]), F32),
               pltpu.VMEM((CAST_SLOTS, w2.shape[0] // CAST_CHUNKS, w2.shape[1]), F32),
               pltpu.SemaphoreType.DMA((2, CAST_SLOTS))]
    if final:
        out_specs = pl.BlockSpec((1, tm, d), tok)
        out_shape = jax.ShapeDtypeStruct((b, l, d), F32)
    else:
        out_specs = [pl.BlockSpec((1, tm, d), tok), pl.BlockSpec((1, tm, d), tok)]
        out_shape = [jax.ShapeDtypeStruct((b, l, d), F32), jax.ShapeDtypeStruct((b, l, d), BF16)]
    return pl.pallas_call(
        functools.partial(_ffn_kernel, row0=row0, final=final),
        grid=(b, l // tm),
        in_specs=in_specs,
        out_specs=out_specs,
        out_shape=out_shape,
        scratch_shapes=scratch,
        compiler_params=_params(("arbitrary", "arbitrary")),
        name="ffn_final" if final else "ffn_mix",
    )(x, mods, norm_w.reshape(1, d), w1, w2, norm2_w.reshape(1, d))


def _ctx_state_kernel(lgf_ref, lgb_ref, u_ref, wk_ref, wv_ref, rf_ref, rb_ref):
    h = pl.program_id(1)
    u = u_ref[0]
    n = u.shape[0]
    k = jnp.dot(u, wk_ref[...].astype(BF16), preferred_element_type=F32) * (RET_QK_DIM ** -0.5)
    v = jnp.dot(u, wv_ref[...].astype(BF16), preferred_element_type=F32).astype(BF16)
    pos = lax.broadcasted_iota(jnp.int32, (n, 1), 0).astype(F32)
    wf = jnp.exp(lgf_ref[h] * ((n - 1.0) - pos))
    wb = jnp.exp(lgb_ref[h] * pos)
    tdot = lambda a, c: lax.dot_general(a, c, (((0,), (0,)), ((), ())), preferred_element_type=F32)
    rf_ref[0, 0] = tdot((k * wf).astype(BF16), v)
    rb_ref[0, 0] = tdot((k * wb).astype(BF16), v)


def _ctx_state(uc, w_in, lgf, lgb):
    b, n, d = uc.shape
    smem = pl.BlockSpec(memory_space=pltpu.SMEM)
    st = jax.ShapeDtypeStruct((b, RET_HEADS, RET_QK_DIM, RET_V_DIM), F32)
    st_spec = pl.BlockSpec((1, 1, RET_QK_DIM, RET_V_DIM), lambda i, h: (i, h, 0, 0))
    return pl.pallas_call(
        _ctx_state_kernel,
        grid=(b, RET_HEADS),
        in_specs=[smem, smem,
                  pl.BlockSpec((1, n, d), lambda i, h: (i, 0, 0)),
                  pl.BlockSpec((d, RET_QK_DIM), lambda i, h: (0, K0 // RET_QK_DIM + h)),
                  pl.BlockSpec((d, RET_V_DIM), lambda i, h: (0, V0 // RET_V_DIM + h))],
        out_specs=[st_spec, st_spec],
        out_shape=[st, st],
        compiler_params=_params(("parallel", "parallel")),
        name="ctx_state",
    )(lgf, lgb, uc, w_in, w_in)


def _depthwise_conv(ybuf, cw_ref, cbuf, c0, r0, zero):
    rows = CONV_RB + HALO
    base = BF16_ROWS - CONV_PAD
    cs = slice(c0, c0 + LANES)
    src = ybuf[r0:r0 + rows, cs]
    parts = [None] * (HALO // SUBLANES)
    for r in range(SUBLANES):
        zr = src if r == 0 else pltpu.roll(src, rows - r, 0)
        for ai, a0 in enumerate(range(0, HALO, SUBLANES)):
            j = a0 + r - base
            if 0 <= j < CONV_WIDTH:
                term = zr[a0:a0 + CONV_RB, :] * cw_ref[j:j + 1, cs]
                parts[ai] = term if parts[ai] is None else parts[ai] + term
    acc = (parts[0] + parts[1]) + (parts[2] + parts[3])
    cbuf[r0:r0 + CONV_RB, cs] = acc + jnp.tile(zero, (CONV_RB // SUBLANES, 1))


def _inproj_kernel(u_ref, uprev_ref, unext_ref, w_hbm, rcos_ref, rsin_ref, ccos_ref, csin_ref,
                   cw_ref, cb_ref, lnw_ref, lnb_ref,
                   q_ref, k_ref, v_ref, rg_ref, ys_ref, ga_ref, gb_ref,
                   uext, ybuf, cbuf, w_ref, stage, sems):
    @pl.when(_first_step())
    def _():
        _load_cast_bf16(w_hbm, w_ref, stage, sems, scaled_cols=(K0, V0, RET_QK_DIM ** -0.5))

    t = pl.program_id(1)
    tm = u_ref.shape[1]
    hb = BF16_ROWS
    u = u_ref[0]

    def proj(c0):
        return jnp.dot(u, w_ref[:, c0:c0 + INPROJ_TN], preferred_element_type=F32)

    def rope(val, ref, o0):
        for gi in range(tm // GRID_W):
            rs = slice(gi * GRID_W, (gi + 1) * GRID_W)
            for l0 in range(0, INPROJ_TN, LANES):
                if (o0 + l0) % RET_QK_DIM == 0:
                    cs, sn = rcos_ref[0, gi:gi + 1, :], rsin_ref[0, gi:gi + 1, :]
                else:
                    cs, sn = ccos_ref[...], csin_ref[...]
                slab = val[rs, l0:l0 + LANES]
                ref[0, rs, o0 + l0:o0 + l0 + LANES] = (
                    slab * cs + pltpu.roll(slab, LANES // 2, 1) * sn).astype(BF16)

    def piece(c0):
        val = proj(c0)
        if c0 < K0:
            rope(val, q_ref, c0)
        elif c0 < V0:
            rope(val, k_ref, c0 - K0)
        elif c0 < G0:
            v_ref[0, :, c0 - V0:c0 - V0 + INPROJ_TN] = val.astype(BF16)
        elif c0 < C0:
            rg_ref[0, :, c0 - G0:c0 - G0 + INPROJ_TN] = (val * jax.nn.sigmoid(val)).astype(BF16)
        elif c0 < GB0:
            ga_ref[0, :, c0 - GA0:c0 - GA0 + INPROJ_TN] = jax.nn.sigmoid(val).astype(BF16)
        else:
            gb_ref[0, :, c0 - GB0:c0 - GB0 + INPROJ_TN] = jax.nn.sigmoid(val).astype(BF16)
        return val

    uext[0:hb, :] = jnp.where(t > 0, uprev_ref[0], jnp.zeros_like(uprev_ref[0]))
    uext[hb:hb + tm, :] = u
    uext[hb + tm:, :] = jnp.where(t < pl.num_programs(1) - 1, unext_ref[0], jnp.zeros_like(unext_ref[0]))
    ue = uext[...]
    ca = jnp.dot(ue, w_ref[:, C0:C0 + D_MODEL], preferred_element_type=F32)
    cg = jnp.dot(ue, w_ref[:, C0 + D_MODEL:GA0], preferred_element_type=F32)
    ybuf[...] = ca * jax.nn.sigmoid(cg)

    cols = [c0 for c0 in range(0, IN_WIDTH, INPROJ_TN) if not C0 <= c0 < GA0]
    blocks = [(c0, r0) for c0 in range(0, D_MODEL, LANES) for r0 in range(0, tm, CONV_RB)]
    per_tile = len(blocks) // len(cols)
    assert per_tile * len(cols) == len(blocks)
    zero = jnp.zeros((SUBLANES, LANES), F32)
    for i, c0 in enumerate(cols):
        for b0, r0 in blocks[i * per_tile:(i + 1) * per_tile]:
            _depthwise_conv(ybuf, cw_ref, cbuf, b0, r0, zero)
        bits = piece(c0)[0:SUBLANES, 0:LANES].astype(jnp.int32)
        zero = lax.shift_right_logical(lax.shift_right_logical(bits, 16), 16).astype(F32)

    yn = _layernorm(cbuf[...] + cb_ref[...], lnw_ref[...], lnb_ref[...])
    ys_ref[0] = (yn * jax.nn.sigmoid(yn)).astype(BF16)


def _inproj(u, w_in, rope, conv_w, conv_b, ln_w, ln_b, *, tm):
    b, l, d = u.shape
    hb = BF16_ROWS
    gpt = tm // GRID_W
    rcos, rsin, ccos, csin = rope
    tok = lambda i, t: (i, t, 0)
    prev = lambda i, t: (i, jnp.maximum(t * (tm // hb) - 1, 0), 0)
    nxt = lambda i, t: (i, jnp.minimum((t + 1) * (tm // hb), l // hb - 1), 0)
    rtab = lambda i, t: (t, 0, 0)
    vec = lambda a: a.reshape(1, d)
    widths = (RET_QK_W, RET_QK_W, RET_V_W, RET_V_W, D_MODEL, D_MODEL, D_MODEL)
    return pl.pallas_call(
        _inproj_kernel,
        grid=(b, l // tm),
        in_specs=[pl.BlockSpec((1, tm, d), tok),
                  pl.BlockSpec((1, hb, d), prev),
                  pl.BlockSpec((1, hb, d), nxt),
                  pl.BlockSpec(memory_space=pl.ANY),
                  pl.BlockSpec((1, gpt, LANES), rtab),
                  pl.BlockSpec((1, gpt, LANES), rtab),
                  _resident(ccos.shape),
                  _resident(csin.shape),
                  _resident(conv_w.shape),
                  _resident((1, d)), _resident((1, d)), _resident((1, d))],
        out_specs=[pl.BlockSpec((1, tm, w), tok) for w in widths],
        out_shape=[jax.ShapeDtypeStruct((b, l, w), BF16) for w in widths],
        scratch_shapes=[pltpu.VMEM((tm + HALO, d), BF16),
                        pltpu.VMEM((tm + HALO, d), F32),
                        pltpu.VMEM((tm, d), F32),
                        pltpu.VMEM(w_in.shape, BF16),
                        pltpu.VMEM((CAST_SLOTS, d // (2 * CAST_CHUNKS), IN_WIDTH), F32),
                        pltpu.SemaphoreType.DMA((CAST_SLOTS,))],
        compiler_params=_params(("arbitrary", "arbitrary")),
        name="inproj",
    )(u, u, u, w_in, rcos.reshape(l // tm, gpt, LANES), rsin.reshape(l // tm, gpt, LANES),
      ccos, csin, conv_w, vec(conv_b), vec(ln_w), vec(ln_b))


def _rope_tables(l):
    half = RET_QK_DIM // 2
    inv = ROPE_BASE ** (-jnp.arange(0, half, 2, dtype=F32) / half)

    def tabs(pos):
        ang = pos[:, None] * inv[None, :]
        return (jnp.concatenate([jnp.cos(ang), jnp.cos(ang)], axis=-1),
                jnp.concatenate([-jnp.sin(ang), jnp.sin(ang)], axis=-1))

    rcos, rsin = tabs(jnp.arange(l // GRID_W, dtype=F32))
    ccos, csin = tabs(jnp.arange(GRID_W, dtype=F32))
    return rcos, rsin, ccos, csin


def _ret_kernel(*refs, reverse):
    if reverse:
        lg_ref, q_ref, k_ref, v_ref, s0_ref, o_ref, state, ubuf = refs
    else:
        lg_ref, lgb_ref, q_ref, k_ref, v_ref, s0_ref, part_ref, gnw_ref, o_ref, state, ubuf, ibuf = refs
    h = pl.program_id(1)

    @pl.when(pl.program_id(2) == 0)
    def _():
        state[...] = s0_ref[0, 0]

    c = RET_SUB
    lg = lg_ref[h]
    pos = lax.broadcasted_iota(jnp.int32, (c, 1), 0).astype(F32)
    if reverse:
        qdec = jnp.exp(lg * (c - pos))
        kdec = jnp.exp(lg * pos)
    else:
        diff = (lax.broadcasted_iota(jnp.int32, (c, c), 0)
                - lax.broadcasted_iota(jnp.int32, (c, c), 1)).astype(F32)
        dmat = jnp.where(diff < 0, jnp.exp(lgb_ref[h] * jnp.maximum(-diff, 0.0)),
                         jnp.exp(lg * jnp.maximum(diff, 0.0)))
        qdec = jnp.exp(lg * (pos + 1.0))
        kdec = jnp.exp(lg * ((c - 1.0) - pos))
    cdec = jnp.exp(lg * jnp.full((1, 1), float(c), F32))

    nchunk = q_ref.shape[1] // c
    order = list(reversed(range(nchunk))) if reverse else list(range(nchunk))

    def local_terms(ci):
        rows = slice(ci * c, (ci + 1) * c)
        kc = k_ref[0, rows, :]
        vc = v_ref[0, rows, :]
        if not reverse:
            s = lax.dot_general(q_ref[0, rows, :], kc, (((1,), (1,)), ((), ())), preferred_element_type=F32)
            ibuf[rows, :] = jnp.dot((s * dmat).astype(BF16), vc, preferred_element_type=F32)
        kd = (kc.astype(F32) * kdec).astype(BF16)
        ubuf[ci] = lax.dot_general(kd, vc, (((0,), (0,)), ((), ())), preferred_element_type=F32)

    local_terms(order[0])
    st = state[...]
    for n, ci in enumerate(order):
        if n + 1 < nchunk:
            local_terms(order[n + 1])
        rows = slice(ci * c, (ci + 1) * c)
        cross = jnp.dot(q_ref[0, rows, :], st.astype(BF16), preferred_element_type=F32) * qdec
        st = st * cdec + ubuf[ci]
        if ci == order[-1]:
            state[...] = st
        if reverse:
            o_ref[0, rows, :] = cross.astype(BF16)
        else:
            o = part_ref[0, rows, :].astype(F32) + (ibuf[rows, :] + cross)
            mu = jnp.mean(o, axis=-1, keepdims=True)
            dlt = o - mu
            var = jnp.mean(dlt * dlt, axis=-1, keepdims=True)
            o_ref[0, rows, :] = ((dlt * lax.rsqrt(var + EPS)) * gnw_ref[...]).astype(BF16)


def _retention(lgs, q, k, v, s0, extra, *, reverse, t):
    b, l, _ = q.shape
    nt = l // t
    tmap = (lambda i, h, j: (i, nt - 1 - j, h)) if reverse else (lambda i, h, j: (i, j, h))
    in_specs = [pl.BlockSpec(memory_space=pltpu.SMEM)] * len(lgs)
    in_specs += [pl.BlockSpec((1, t, RET_QK_DIM), tmap),
                 pl.BlockSpec((1, t, RET_QK_DIM), tmap),
                 pl.BlockSpec((1, t, RET_V_DIM), tmap),
                 pl.BlockSpec((1, 1, RET_QK_DIM, RET_V_DIM), lambda i, h, j: (i, h, 0, 0))]
    scratch = [pltpu.VMEM((RET_QK_DIM, RET_V_DIM), F32),
               pltpu.VMEM((t // RET_SUB, RET_QK_DIM, RET_V_DIM), F32)]
    if not reverse:
        scratch.append(pltpu.VMEM((t, RET_V_DIM), F32))
        in_specs += [pl.BlockSpec((1, t, RET_V_DIM), tmap),
                     pl.BlockSpec((1, RET_V_DIM), lambda i, h, j: (0, h))]
    return pl.pallas_call(
        functools.partial(_ret_kernel, reverse=reverse),
        grid=(b, RET_HEADS, nt),
        in_specs=in_specs,
        out_specs=pl.BlockSpec((1, t, RET_V_DIM), tmap),
        out_shape=jax.ShapeDtypeStruct((b, l, RET_V_W), BF16),
        scratch_shapes=scratch,
        compiler_params=_params(("parallel", "parallel", "arbitrary")),
        name="ret_bwd" if reverse else "ret_fwd",
    )(*lgs, q, k, v, s0, *extra)


def _mixer_kernel(on_ref, rg_ref, ys_ref, ga_ref, gb_ref, h_ref, mods_ref, wro_hbm, wco_hbm, wo_hbm, o_ref,
                  wro_ref, wco_ref, wo_ref, stage, sems):
    @pl.when(_first_step())
    def _():
        _load_cast_bf16(wro_hbm, wro_ref, stage, sems)
        _load_cast_bf16(wco_hbm, wco_ref, stage, sems)
        _load_cast_bf16(wo_hbm, wo_ref, stage, sems)

    y_conv = jnp.dot(ys_ref[0], wco_ref[...], preferred_element_type=F32)
    gated = (rg_ref[0].astype(F32) * on_ref[0].astype(F32)).astype(BF16)
    y_ret = jnp.dot(gated, wro_ref[...], preferred_element_type=F32)
    merged = (ga_ref[0].astype(F32) * y_ret + gb_ref[0].astype(F32) * y_conv).astype(BF16)
    m = jnp.dot(merged, wo_ref[...], preferred_element_type=F32)
    g2 = mods_ref[0, 5:6, :]
    o_ref[0] = h_ref[0] + g2 * m


def _mixer(on, srg, ys, sga, sgb, h, mods, wro, wco, wo, *, tm):
    b, l, d = h.shape
    tok = lambda i, t: (i, t, 0)
    hbm = pl.BlockSpec(memory_space=pl.ANY)
    return pl.pallas_call(
        _mixer_kernel,
        grid=(b, l // tm),
        in_specs=[pl.BlockSpec((1, tm, RET_V_W), tok),
                  pl.BlockSpec((1, tm, RET_V_W), tok),
                  pl.BlockSpec((1, tm, d), tok),
                  pl.BlockSpec((1, tm, d), tok),
                  pl.BlockSpec((1, tm, d), tok),
                  pl.BlockSpec((1, tm, d), tok),
                  pl.BlockSpec((1, N_MOD, d), lambda i, t: (i, 0, 0)),
                  hbm, hbm, hbm],
        out_specs=pl.BlockSpec((1, tm, d), tok),
        out_shape=jax.ShapeDtypeStruct((b, l, d), F32),
        scratch_shapes=[pltpu.VMEM(wro.shape, BF16),
                        pltpu.VMEM(wco.shape, BF16),
                        pltpu.VMEM(wo.shape, BF16),
                        pltpu.VMEM((CAST_SLOTS, d // (CAST_CHUNKS // 2), d), F32),
                        pltpu.SemaphoreType.DMA((CAST_SLOTS,))],
        compiler_params=_params(("arbitrary", "arbitrary")),
        name="mixer",
    )(on, srg, ys, sga, sgb, h, mods, wro, wco, wo)


def kernel(x, c, ctx, c_ctx, w_mod, b_mod, norm_ffn1, w_ffn1_in, w_ffn1_out, norm_mix, w_in, ret_decay_f, ret_decay_b, ret_gn_w, w_ret_out, conv_w, conv_b, conv_ln_w, conv_ln_b, w_conv_out, w_out, norm_ffn2, w_ffn2_in, w_ffn2_out, final_norm):
    assert w_mod.shape[0] == 1, "single-layer block"
    b, l, d = x.shape
    assert l % max(FFN_TM, RET_T, MIX_TM) == 0 and d == D_MODEL

    cc = jnp.zeros((SUBLANES, d), F32).at[:b].set(c).at[b].set(c_ctx)
    mods_all = _mods(cc, w_mod[0], b_mod[0])
    mods = mods_all[:b].reshape(b, N_MOD, d)
    mods_c = mods_all[b:b + 1].reshape(1, N_MOD, d)

    lgf = jax.nn.log_sigmoid(ret_decay_f[0].astype(F32))
    lgb = jax.nn.log_sigmoid(ret_decay_b[0].astype(F32))

    h, u = _ffn(x, mods, norm_ffn1[0], w_ffn1_in[0], w_ffn1_out[0], norm_mix[0], row0=0, final=False, tm=FFN_TM)
    n_ctx = ctx.shape[1]
    _, uc = _ffn(ctx.reshape(1, b * n_ctx, d), mods_c, norm_ffn1[0], w_ffn1_in[0], w_ffn1_out[0], norm_mix[0],
                 row0=0, final=False, tm=FFN_TM)

    rf, rb = _ctx_state(uc.reshape(b, n_ctx, d), w_in[0], lgf, lgb)

    q, k, v, srg, ys, sga, sgb = _inproj(u, w_in[0], _rope_tables(l), conv_w[0], conv_b[0],
                                         conv_ln_w[0], conv_ln_b[0], tm=INPROJ_TM)

    part = _retention((lgb,), q, k, v, rb, (), reverse=True, t=RET_T)
    on = _retention((lgf, lgb), q, k, v, rf, (part, ret_gn_w[0].reshape(1, RET_V_W)), reverse=False, t=RET_T)

    h2 = _mixer(on, srg, ys, sga, sgb, h, mods, w_ret_out[0], w_conv_out[0], w_out[0], tm=MIX_TM)

    return _ffn(h2, mods, norm_ffn2[0], w_ffn2_in[0], w_ffn2_out[0], final_norm, row0=6, final=True, tm=FFN_TM)
```
